```python
import math
import jax, jax.numpy as jnp
from jax import lax
import numpy as np

D_MODEL = 4096
BATCH = 2
SEQ = 8192
DEPTH = 2

GRID_W = 64
CTX_LEN = 256
HEAD_DIM = 128
RET_HEADS = 8
RET_DK = 128
RET_DV = 128
RET_CHUNK = 128
SWA_Q_HEADS = 12
SWA_KV_HEADS = 4
SWA_WINDOW = 128
SWA_BLOCK = 128
DIFF_HEADS = 12
DIFF_QK_DIM = 64
DIFF_V_DIM = 2 * DIFF_QK_DIM
DIFF_Q_BLOCK = 128
N_EXPERTS = 32
TOP_K = 4
EXPERT_FF = 640
SWIGLU_LIMIT = 7.0
SWIGLU_ALPHA = 1.702
ROPE_THETA = 10000.0
LN_EPS = 1e-5
NEG_INF = -1e30
DEEPNORM_ALPHA = (2 * DEPTH) ** 0.25
DEEPNORM_BETA = (8 * DEPTH) ** -0.25

RET_W = RET_HEADS * RET_DV
SWA_W = SWA_Q_HEADS * HEAD_DIM
DIFF_W = DIFF_HEADS * DIFF_V_DIM
MIX_W = RET_W + SWA_W + DIFF_W
IN_WIDTHS = (
    RET_HEADS * RET_DK, RET_HEADS * RET_DK, RET_W, RET_W,
    SWA_Q_HEADS * HEAD_DIM, SWA_KV_HEADS * HEAD_DIM, SWA_KV_HEADS * HEAD_DIM,
    2 * DIFF_HEADS * DIFF_QK_DIM, 2 * DIFF_HEADS * DIFF_QK_DIM, DIFF_W,
)
IN_W = sum(IN_WIDTHS)
V_SEGMENTS = (2, 6, 9)

kernel_name = 'hybrid_retention_swa_diffattn_moe_dit'


def layer_norm(x, g, b):
    xf = x.astype(jnp.float32)
    mu = jnp.mean(xf, axis=-1, keepdims=True)
    var = jnp.mean(jnp.square(xf - mu), axis=-1, keepdims=True)
    return ((xf - mu) * lax.rsqrt(var + LN_EPS)).astype(x.dtype) * g + b


def split_heads(t, n_heads):
    b, n, _ = t.shape
    return t.reshape(b, n, n_heads, -1).transpose(0, 2, 1, 3)


def merge_heads(t):
    b, h, n, d = t.shape
    return t.transpose(0, 2, 1, 3).reshape(b, n, h * d)


def split_diff_heads(t):
    b, n, _ = t.shape
    return t.reshape(b, n, DIFF_HEADS, 2, DIFF_QK_DIM).transpose(0, 2, 3, 1, 4)


def split_projection(p):
    offsets = [int(o) for o in np.cumsum(IN_WIDTHS)[:-1]]
    return jnp.split(p, offsets, axis=-1)


def rope_axis(x, pos):
    half = x.shape[-1] // 2
    freqs = ROPE_THETA ** (-jnp.arange(half, dtype=jnp.float32) / half)
    ang = pos.astype(jnp.float32)[:, None] * freqs[None, :]
    cos, sin = jnp.cos(ang).astype(x.dtype), jnp.sin(ang).astype(x.dtype)
    x1, x2 = x[..., :half], x[..., half:]
    return jnp.concatenate([x1 * cos - x2 * sin, x1 * sin + x2 * cos], axis=-1)


def rope_2d(x, row, col):
    half = x.shape[-1] // 2
    return jnp.concatenate([rope_axis(x[..., :half], row), rope_axis(x[..., half:], col)], axis=-1)


def retention_chunkwise(q, k, v, log_gamma, s0, strict):
    b, h, n, dk = q.shape
    dv = v.shape[-1]
    nc = n // RET_CHUNK
    idx = jnp.arange(RET_CHUNK, dtype=jnp.float32)
    rel = idx[:, None] - idx[None, :]
    lg = log_gamma[:, None, None]
    inside = rel > 0 if strict else rel >= 0
    dmask = jnp.where(inside, jnp.exp(lg * jnp.maximum(rel, 0.0)), 0.0).astype(q.dtype)
    xi = jnp.exp(log_gamma[:, None] * (idx + 1.0))[:, :, None].astype(q.dtype)
    zeta = jnp.exp(log_gamma[:, None] * (RET_CHUNK - 1.0 - idx))[:, :, None].astype(q.dtype)
    g_chunk = jnp.exp(log_gamma * RET_CHUNK)[:, None, None].astype(q.dtype)

    def chunks(t):
        return jnp.moveaxis(t.reshape(b, h, nc, RET_CHUNK, t.shape[-1]), 2, 0)

    def step(s, qkv):
        qi, ki, vi = qkv
        scores = jnp.einsum('bhid,bhjd->bhij', qi, ki) * dmask
        o = jnp.einsum('bhij,bhje->bhie', scores, vi) + jnp.einsum('bhid,bhde->bhie', qi, s) * xi
        s = g_chunk * s + jnp.einsum('bhjd,bhje->bhde', ki * zeta, vi)
        return s, o

    s_final, o = lax.scan(step, s0, (chunks(q), chunks(k), chunks(v)))
    return jnp.moveaxis(o, 0, 2).reshape(b, h, n, dv), s_final


def retention_context_state(kc, vc, log_gamma, reverse):
    c = kc.shape[2]
    idx = jnp.arange(c, dtype=jnp.float32)
    dist = idx if reverse else (c - 1.0 - idx)
    w = jnp.exp(log_gamma[:, None] * dist[None, :]).astype(kc.dtype)
    return jnp.einsum('bhcd,hc,bhce->bhde', kc, w, vc)


def retention_bidir(q, k, v, lg_f, lg_b, s_f, s_b):
    o_f, _ = retention_chunkwise(q, k, v, lg_f, s_f, False)
    flip = lambda t: jnp.flip(t, axis=2)
    o_b, _ = retention_chunkwise(flip(q), flip(k), flip(v), lg_b, s_b, True)
    return o_f + flip(o_b)


def retention_output(o, gate, norm_w):
    of = o.astype(jnp.float32)
    mu = jnp.mean(of, axis=-1, keepdims=True)
    var = jnp.mean(jnp.square(of - mu), axis=-1, keepdims=True)
    on = ((of - mu) * lax.rsqrt(var + LN_EPS)).astype(o.dtype) * norm_w.reshape(RET_HEADS, 1, RET_DV)
    return merge_heads(on) * jax.nn.silu(gate)


def retention_mixer(q, k, v, g, qc, kc, vc, gc, a_f, a_b, norm_w, ctx_out):
    lg_f = jax.nn.log_sigmoid(a_f.astype(jnp.float32))
    lg_b = jax.nn.log_sigmoid(a_b.astype(jnp.float32))
    k = k * RET_DK ** -0.5
    kc = kc * RET_DK ** -0.5
    s_f = retention_context_state(kc, vc, lg_f, False)
    s_b = retention_context_state(kc, vc, lg_b, True)
    y = retention_output(retention_bidir(q, k, v, lg_f, lg_b, s_f, s_b), g, norm_w)
    yc = None
    if ctx_out:
        zero = jnp.zeros_like(s_f)
        yc = retention_output(retention_bidir(qc, kc, vc, lg_f, lg_b, zero, zero), gc, norm_w)
    return y, yc


def swa_mixer(q, k, v, qc, kc, vc, sink, ctx_out):
    b, hq, n, d = q.shape
    hkv = k.shape[1]
    grp = hq // hkv
    w = SWA_BLOCK
    nb = n // w
    c = kc.shape[2]
    scale = d ** -0.5
    sink_l = sink.astype(jnp.float32).reshape(1, hkv, grp, 1, 1)

    def banded(t):
        tp = jnp.pad(t, ((0, 0), (0, 0), (w, w), (0, 0))).reshape(b, hkv, nb + 2, w, d)
        return jnp.concatenate([tp[:, :, :nb], tp[:, :, 1:nb + 1], tp[:, :, 2:]], axis=3)

    kb, vb = banded(k), banded(v)
    qb = q.reshape(b, hkv, grp, nb, w, d)
    s_loc = jnp.einsum('bhgnqd,bhnkd->bhgnqk', qb, kb).astype(jnp.float32) * scale
    blk = jnp.arange(nb)[:, None]
    qpos = blk * w + jnp.arange(w)[None, :]
    kpos = blk * w - w + jnp.arange(3 * w)[None, :]
    dist = kpos[:, None, :] - qpos[:, :, None]
    valid = (jnp.abs(dist) <= SWA_WINDOW) & (kpos[:, None, :] >= 0) & (kpos[:, None, :] < n)
    s_loc = jnp.where(valid, s_loc, NEG_INF)
    s_ctx = jnp.einsum('bhgnqd,bhcd->bhgnqc', qb, kc).astype(jnp.float32) * scale
    sink_col = jnp.broadcast_to(sink_l[..., None], s_loc.shape[:-1] + (1,))
    p = jax.nn.softmax(jnp.concatenate([s_loc, s_ctx, sink_col], axis=-1), axis=-1).astype(v.dtype)
    o = (jnp.einsum('bhgnqk,bhnkd->bhgnqd', p[..., :3 * w], vb)
         + jnp.einsum('bhgnqc,bhcd->bhgnqd', p[..., 3 * w:3 * w + c], vc))
    y = merge_heads(o.reshape(b, hq, n, d))
    yc = None
    if ctx_out:
        qcb = qc.reshape(b, hkv, grp, c, d)
        s = jnp.einsum('bhgqd,bhcd->bhgqc', qcb, kc).astype(jnp.float32) * scale
        s = jnp.concatenate([s, jnp.broadcast_to(sink_l, s.shape[:-1] + (1,))], axis=-1)
        pc = jax.nn.softmax(s, axis=-1).astype(vc.dtype)
        yc = merge_heads(jnp.einsum('bhgqc,bhcd->bhgqd', pc[..., :c], vc).reshape(b, hq, c, d))
    return y, yc


def diff_attend(qi, k_all, v_all, lam):
    s = jnp.einsum('bhmqd,bhmkd->bhmqk', qi, k_all).astype(jnp.float32) * DIFF_QK_DIM ** -0.5
    p = jax.nn.softmax(s, axis=-1)
    a = p[:, :, 0] - lam * p[:, :, 1]
    return jnp.einsum('bhqk,bhkd->bhqd', a.astype(v_all.dtype), v_all)


def diff_output(o, norm_w, lambda_init):
    of = o.astype(jnp.float32)
    on = of * lax.rsqrt(jnp.mean(jnp.square(of), axis=-1, keepdims=True) + LN_EPS)
    return merge_heads(on.astype(o.dtype) * norm_w * (1.0 - lambda_init))


def diff_mixer(q, k, v, qc, kc, vc, lam_params, norm_w, lambda_init, ctx_out):
    lp = lam_params.astype(jnp.float32)
    lam = jnp.exp(jnp.sum(lp[0] * lp[1])) - jnp.exp(jnp.sum(lp[2] * lp[3])) + lambda_init
    b, h, _, n, dq = q.shape
    nb = n // DIFF_Q_BLOCK
    k_all = jnp.concatenate([k, kc], axis=3)
    v_all = jnp.concatenate([v, vc], axis=2)
    q_blocks = jnp.moveaxis(q.reshape(b, h, 2, nb, DIFF_Q_BLOCK, dq), 3, 0)
    o = lax.map(lambda qi: diff_attend(qi, k_all, v_all, lam), q_blocks)
    o = jnp.moveaxis(o, 0, 2).reshape(b, h, n, DIFF_V_DIM)
    y = diff_output(o, norm_w, lambda_init)
    yc = diff_output(diff_attend(qc, kc, vc, lam), norm_w, lambda_init) if ctx_out else None
    return y, yc


def mixer_block(h, hc, w_in, w_out, ret_a_f, ret_a_b, ret_norm_w, swa_sink, diff_lam, diff_norm_w,
                lambda_init, row, col, ctx_out):
    rq, rk, rv, rg, sq, sk, sv, dq, dk, dv = split_projection(h @ w_in)
    rqc, rkc, rvc, rgc, sqc, skc, svc, dqc, dkc, dvc = split_projection(hc @ w_in)
    ret_y, ret_yc = retention_mixer(
        split_heads(rq, RET_HEADS), split_heads(rk, RET_HEADS), split_heads(rv, RET_HEADS), rg,
        split_heads(rqc, RET_HEADS), split_heads(rkc, RET_HEADS), split_heads(rvc, RET_HEADS), rgc,
        ret_a_f, ret_a_b, ret_norm_w, ctx_out)
    swa_y, swa_yc = swa_mixer(
        rope_2d(split_heads(sq, SWA_Q_HEADS), row, col), rope_2d(split_heads(sk, SWA_KV_HEADS), row, col),
        split_heads(sv, SWA_KV_HEADS),
        split_heads(sqc, SWA_Q_HEADS), split_heads(skc, SWA_KV_HEADS), split_heads(svc, SWA_KV_HEADS),
        swa_sink, ctx_out)
    diff_y, diff_yc = diff_mixer(
        rope_2d(split_diff_heads(dq), row, col), rope_2d(split_diff_heads(dk), row, col),
        split_heads(dv, DIFF_HEADS),
        split_diff_heads(dqc), split_diff_heads(dkc), split_heads(dvc, DIFF_HEADS),
        diff_lam, diff_norm_w, lambda_init, ctx_out)
    y = jnp.concatenate([ret_y, swa_y, diff_y], axis=-1) @ w_out
    yc = jnp.concatenate([ret_yc, swa_yc, diff_yc], axis=-1) @ w_out if ctx_out else None
    return y, yc


def moe_ffn(h, w_router, b_router, w_gate_up, b_gate_up, w_down, b_down):
    logits = (h @ w_router + b_router).astype(jnp.float32)
    top_logits, top_idx = lax.top_k(logits, TOP_K)
    top_w = jax.nn.softmax(top_logits, axis=-1)
    gates = jnp.sum(jax.nn.one_hot(top_idx, N_EXPERTS, dtype=jnp.float32) * top_w[..., None],
                    axis=-2).astype(h.dtype)
    out = jnp.zeros_like(h)
    for e in range(N_EXPERTS):
        gu = h @ w_gate_up[e] + b_gate_up[e]
        gate = jnp.minimum(gu[..., 0::2], SWIGLU_LIMIT)
        up = jnp.clip(gu[..., 1::2], -SWIGLU_LIMIT, SWIGLU_LIMIT)
        act = (up + 1.0) * gate * jax.nn.sigmoid(SWIGLU_ALPHA * gate)
        out = out + gates[..., e:e + 1] * (act @ w_down[e] + b_down[e])
    return out


def setup_inputs(seed: int = 0) -> dict:
    key = jax.random.key(seed)
    ks = jax.random.split(key, 24)
    L, D, E, F = DEPTH, D_MODEL, N_EXPERTS, EXPERT_FF
    nrm = lambda k, shape: jax.random.normal(k, shape, dtype=jnp.float32)
    in_scale = jnp.concatenate([
        jnp.full((wd,), DEEPNORM_BETA if i in V_SEGMENTS else 1.0, dtype=jnp.float32)
        for i, wd in enumerate(IN_WIDTHS)])
    gamma = 1.0 - 2.0 ** (-5.0 - np.arange(RET_HEADS))
    decay_logit = jnp.asarray(np.log(gamma / (1.0 - gamma)), dtype=jnp.float32)
    return {
        'x': nrm(ks[0], (BATCH, SEQ, D)),
        'c': nrm(ks[1], (BATCH, D)),
        'ctx': nrm(ks[2], (BATCH, CTX_LEN, D)),
        'c_ctx': nrm(ks[3], (D,)),
        'w_ada': nrm(ks[4], (L, D, 6 * D)) * (0.5 * D ** -0.5),
        'b_ada': 0.02 * nrm(ks[5], (L, 6 * D)),
        'w_in': nrm(ks[6], (L, D, IN_W)) * (D ** -0.5) * in_scale,
        'w_out': nrm(ks[7], (L, MIX_W, D)) * (MIX_W ** -0.5 * DEEPNORM_BETA),
        'ret_decay_f': decay_logit + 0.05 * nrm(ks[8], (L, RET_HEADS)),
        'ret_decay_b': decay_logit + 0.05 * nrm(ks[9], (L, RET_HEADS)),
        'ret_norm_w': 1.0 + 0.02 * nrm(ks[10], (L, RET_W)),
        'swa_sink': 0.5 * nrm(ks[11], (L, SWA_Q_HEADS)),
        'diff_lambda': 0.1 * nrm(ks[12], (L, 4, DIFF_QK_DIM)),
        'diff_norm_w': 1.0 + 0.02 * nrm(ks[13], (L, DIFF_V_DIM)),
        'ln1_g': 1.0 + 0.02 * nrm(ks[14], (L, D)),
        'ln1_b': 0.02 * nrm(ks[15], (L, D)),
        'ln2_g': 1.0 + 0.02 * nrm(ks[16], (L, D)),
        'ln2_b': 0.02 * nrm(ks[17], (L, D)),
        'w_router': nrm(ks[18], (L, D, E)) * D ** -0.5,
        'b_router': 0.01 * nrm(ks[19], (L, E)),
        'w_gate_up': nrm(ks[20], (L, E, D, 2 * F)) * D ** -0.5,
        'b_gate_up': 0.01 * nrm(ks[21], (L, E, 2 * F)),
        'w_down': nrm(ks[22], (L, E, F, D)) * (F ** -0.5 * DEEPNORM_BETA),
        'b_down': 0.01 * nrm(ks[23], (L, E, D)),
    }


def reference(x, c, ctx, c_ctx, w_ada, b_ada, w_in, w_out, ret_decay_f, ret_decay_b, ret_norm_w,
              swa_sink, diff_lambda, diff_norm_w, ln1_g, ln1_b, ln2_g, ln2_b,
              w_router, b_router, w_gate_up, b_gate_up, w_down, b_down):
    n = x.shape[1]
    rows = n // GRID_W
    row = jnp.repeat(jnp.arange(rows, dtype=jnp.int32), GRID_W)
    col = jnp.tile(jnp.arange(GRID_W, dtype=jnp.int32), rows)
    xc = ctx
    silu_c = jax.nn.silu(c)
    silu_cc = jax.nn.silu(c_ctx)
    for l in range(DEPTH):
        ctx_out = l < DEPTH - 1
        lambda_init = 0.8 - 0.6 * math.exp(-0.3 * l)
        mod = (silu_c @ w_ada[l] + b_ada[l])[:, None, :]
        mod_c = silu_cc @ w_ada[l] + b_ada[l]
        sh_a, sc_a, g_a, sh_f, sc_f, g_f = jnp.split(mod, 6, axis=-1)
        shc_a, scc_a, gc_a, shc_f, scc_f, gc_f = jnp.split(mod_c, 6, axis=-1)
        h = x * (1.0 + sc_a) + sh_a
        hc = xc * (1.0 + scc_a) + shc_a
        y, yc = mixer_block(h, hc, w_in[l], w_out[l], ret_decay_f[l], ret_decay_b[l], ret_norm_w[l],
                            swa_sink[l], diff_lambda[l], diff_norm_w[l], lambda_init, row, col, ctx_out)
        x = layer_norm(DEEPNORM_ALPHA * x + g_a * y, ln1_g[l], ln1_b[l])
        h = x * (1.0 + sc_f) + sh_f
        moe_args = (w_router[l], b_router[l], w_gate_up[l], b_gate_up[l], w_down[l], b_down[l])
        if ctx_out:
            xc = layer_norm(DEEPNORM_ALPHA * xc + gc_a * yc, ln1_g[l], ln1_b[l])
            hc = xc * (1.0 + scc_f) + shc_f
            f_all = moe_ffn(jnp.concatenate([hc, h], axis=1), *moe_args)
            fc, f = f_all[:, :CTX_LEN], f_all[:, CTX_LEN:]
            xc = layer_norm(DEEPNORM_ALPHA * xc + gc_f * fc, ln2_g[l], ln2_b[l])
        else:
            f = moe_ffn(h, *moe_args)
        x = layer_norm(DEEPNORM_ALPHA * x + g_f * f, ln2_g[l], ln2_b[l])
    return x
```

```python
import functools
import math

import numpy as np
import jax
import jax.numpy as jnp
from jax import lax
from jax.experimental import pallas as pl
from jax.experimental.pallas import tpu as pltpu

_F32 = jnp.float32
_BF = jnp.bfloat16

_GRID_W = 64
_HD = 128
_RET_H = 8
_SWA_Q = 12
_SWA_KV = 4
_SWA_GRP = _SWA_Q // _SWA_KV
_SWA_WIN = 128
_DIFF_H = 12
_DIFF_DQ = 64
_N_EXP = 32
_TOP_K = 4
_LIMIT = 7.0
_SWIGLU_ALPHA = 1.702
_ROPE_THETA = 10000.0
_EPS = 1e-5
_NEG = -1e30
_LOG2E = 1.4426950408889634

_CB_RET_Q, _CB_RET_K, _CB_RET_V, _CB_RET_G = 0, 8, 16, 24
_CB_SWA_Q, _CB_SWA_K, _CB_SWA_V = 32, 44, 48
_CB_DIF_Q, _CB_DIF_K, _CB_DIF_V = 52, 64, 76
_IN_W = 88 * 128

_ROW_TILE = 256
_MOE_TM = 256
_VMEM_CAP = 56 * 2 ** 20


def _params(sem, vmem=None):
    return pltpu.CompilerParams(dimension_semantics=sem, vmem_limit_bytes=vmem)


def _dot(a, b):
    return jnp.dot(a, b, preferred_element_type=_F32)


def _dot_nt(a, b):
    return lax.dot_general(a, b, (((1,), (1,)), ((), ())), preferred_element_type=_F32)


def _dot_tn(a, b):
    return lax.dot_general(a, b, (((0,), (0,)), ((), ())), preferred_element_type=_F32)


def _ada_kernel(ct_ref, w_ref, b_ref, o_ref):
    c = ct_ref[...]
    c = c * jax.nn.sigmoid(c)
    w = w_ref[...]
    rows = [jnp.sum(w * c[:, r:r + 1], axis=0, keepdims=True) for r in range(3)]
    rows.append(jnp.zeros((5, w.shape[1]), _F32))
    o_ref[...] = jnp.concatenate(rows, axis=0) + b_ref[...]


def _ada_mods(c, c_ctx, w_ada, b_ada):
    L, D, W6 = w_ada.shape
    ct = jnp.zeros((D, 8), _F32).at[:, 0:2].set(c.T).at[:, 2].set(c_ctx)
    tn = 512
    out = pl.pallas_call(
        _ada_kernel,
        grid=(L, W6 // tn),
        in_specs=[pl.BlockSpec((D, 8), lambda l, j: (0, 0)),
                  pl.BlockSpec((None, D, tn), lambda l, j: (l, 0, j)),
                  pl.BlockSpec((None, 1, tn), lambda l, j: (l, 0, j))],
        out_specs=pl.BlockSpec((None, 8, tn), lambda l, j: (l, 0, j)),
        out_shape=jax.ShapeDtypeStruct((L, 8, W6), _F32),
        compiler_params=_params(("arbitrary", "arbitrary"), 40 * 2 ** 20),
        name="ada_mods",
    )(ct, w_ada, b_ada.reshape(L, 1, W6))
    return out.reshape(L * 8 * 6, 1, D)


def _mod_index(l, chunk):
    def index(b, t, *_):
        return ((l * 8 + jnp.where(t == 0, 2, b)) * 6 + chunk, 0, 0)
    return index


def _modulate_kernel(x_ref, sc_ref, sh_ref, h_ref):
    h_ref[...] = (x_ref[...] * (1.0 + sc_ref[...]) + sh_ref[...]).astype(h_ref.dtype)


def _modulate(u, modv, l):
    B, S, D = u.shape
    tr = _ROW_TILE
    vec = lambda chunk: pl.BlockSpec((None, 1, D), _mod_index(l, chunk))
    return pl.pallas_call(
        _modulate_kernel,
        grid=(B, S // tr),
        in_specs=[pl.BlockSpec((None, tr, D), lambda b, t: (b, t, 0)), vec(1), vec(0)],
        out_specs=pl.BlockSpec((None, tr, D), lambda b, t: (b, t, 0)),
        out_shape=jax.ShapeDtypeStruct((B, S, D), _BF),
        compiler_params=_params(("parallel", "parallel"), 40 * 2 ** 20),
        name="modulate",
    )(u, modv, modv)


def _rope_tables(S, C):
    n = S - C
    t = np.arange(n)
    row, col = t // _GRID_W, t % _GRID_W
    out_cos, out_sa, out_sb = [], [], []
    for seg in (64, 32):
        half = seg // 2
        lane = np.arange(128)
        j = lane % seg
        pos = np.where((lane // seg) % 2 == 0, row[:, None], col[:, None]).astype(np.float32)
        freqs = (_ROPE_THETA ** (-(np.arange(half, dtype=np.float32)) / half)).astype(np.float32)
        ang = pos * freqs[j % half][None, :]
        cos, sin = np.cos(ang), np.sin(ang)
        first = (j < half)[None, :]
        ident = np.zeros((C, 128), np.float32)
        out_cos.append(np.concatenate([ident + 1.0, cos], 0))
        out_sa.append(np.concatenate([ident, np.where(first, -sin, 0.0)], 0))
        out_sb.append(np.concatenate([ident, np.where(first, 0.0, sin)], 0))
    mk = lambda parts: jnp.asarray(np.stack(parts).astype(np.float32))
    return mk(out_cos), mk(out_sa), mk(out_sb)


def _inproj_kernel(kind_ref, a_ref, w_ref, cos_ref, sa_ref, sb_ref, scale_ref, o_ref, acc_ref):
    j = pl.program_id(2)
    acc_ref[...] = _dot(a_ref[...], w_ref[...])
    kind = kind_ref[j]

    @pl.when(kind == 0)
    def _():
        o_ref[...] = acc_ref[...].astype(o_ref.dtype)

    def rotate(half):
        s = scale_ref[j]
        cos, sa, sb = cos_ref[...] * s, sa_ref[...] * s, sb_ref[...] * s
        for cc in range(o_ref.shape[1] // 128):
            x = acc_ref[:, cc * 128:(cc + 1) * 128]
            y = x * cos + pltpu.roll(x, 128 - half, 1) * sa + pltpu.roll(x, half, 1) * sb
            o_ref[:, cc * 128:(cc + 1) * 128] = y.astype(o_ref.dtype)

    @pl.when(kind == 1)
    def _():
        rotate(32)

    @pl.when(kind == 2)
    def _():
        rotate(16)


def _inproj(h, w_bf, tabs, C):
    B, S, D = h.shape
    tn = 512
    tm = 768 if S % 768 == 0 else 256
    nj = _IN_W // tn
    kinds, scales = np.zeros((nj,), np.int32), np.ones((nj,), np.float32)
    for j in range(nj):
        cb = j * tn // 128
        if _CB_SWA_Q <= cb < _CB_SWA_V:
            kinds[j] = 1
            if cb < _CB_SWA_K:
                scales[j] = _HD ** -0.5 * _LOG2E
        elif _CB_DIF_Q <= cb < _CB_DIF_V:
            kinds[j] = 2
            if cb < _CB_DIF_K:
                scales[j] = _DIFF_DQ ** -0.5 * _LOG2E
    tabsel = lambda b, i, j, kind: (jnp.maximum(kind[j] - 1, 0), i, 0)
    cos, sa, sb = tabs
    grid_spec = pltpu.PrefetchScalarGridSpec(
        num_scalar_prefetch=1,
        grid=(B, S // tm, nj),
        in_specs=[pl.BlockSpec((None, tm, D), lambda b, i, j, kind: (b, i, 0)),
                  pl.BlockSpec((D, tn), lambda b, i, j, kind: (0, j)),
                  pl.BlockSpec((None, tm, 128), tabsel),
                  pl.BlockSpec((None, tm, 128), tabsel),
                  pl.BlockSpec((None, tm, 128), tabsel),
                  pl.BlockSpec(memory_space=pltpu.SMEM)],
        out_specs=pl.BlockSpec((None, tm, tn), lambda b, i, j, kind: (b, i, j)),
        scratch_shapes=[pltpu.VMEM((tm, tn), _F32)])
    return pl.pallas_call(
        _inproj_kernel,
        grid_spec=grid_spec,
        out_shape=jax.ShapeDtypeStruct((B, S, _IN_W), _BF),
        compiler_params=_params(("parallel", "parallel", "arbitrary"), 48 * 2 ** 20),
        name="in_proj",
    )(jnp.asarray(kinds), h, w_bf, cos, sa, sb, jnp.asarray(scales))


def _retention_kernel(lg_ref, q_ref, k_ref, v_ref, g_ref, nw_ref, o_ref, kvf_ref, kvb_ref, *, n_ctx_chunks):
    h = pl.program_id(1)
    nc = kvf_ref.shape[0]
    ck = 128
    lgf, lgb = lg_ref[0, h], lg_ref[1, h]
    sck = _HD ** -0.5
    ii = lax.broadcasted_iota(jnp.int32, (ck, ck), 0).astype(_F32)
    jj = lax.broadcasted_iota(jnp.int32, (ck, ck), 1).astype(_F32)
    rel = ii - jj
    dmask = jnp.where(rel >= 0.0, jnp.exp(lgf * jnp.maximum(rel, 0.0)),
                      jnp.exp(lgb * jnp.maximum(-rel, 0.0))) * sck
    col = lax.broadcasted_iota(jnp.int32, (ck, 1), 0).astype(_F32)
    zeta_f = jnp.exp(lgf * (ck - 1.0 - col)) * sck
    zeta_b = jnp.exp(lgb * col) * sck
    xi_f = jnp.exp(lgf * (col + 1.0))
    xi_b = jnp.exp(lgb * (ck - col))
    g_f = jnp.exp(lgf * ck)
    g_b = jnp.exp(lgb * ck)

    def rows(c):
        return pl.ds(pl.multiple_of(c * ck, ck), ck)

    def chunk_kv(c, _):
        k = k_ref[rows(c), :].astype(_F32)
        v = v_ref[rows(c), :]
        kvf_ref[c] = _dot_tn((k * zeta_f).astype(_BF), v)
        kvb_ref[c] = _dot_tn((k * zeta_b).astype(_BF), v)
        return 0

    lax.fori_loop(0, nc, chunk_kv, 0)

    def fscan(c, s):
        t = kvf_ref[c]
        kvf_ref[c] = s
        return g_f * s + t

    lax.fori_loop(0, nc, fscan, jnp.zeros((ck, ck), _F32))

    def bstep(c, s):
        t = kvb_ref[c]
        kvb_ref[c] = s
        return g_b * s + t

    s = lax.fori_loop(0, n_ctx_chunks, lambda i, s: bstep(n_ctx_chunks - 1 - i, s), jnp.zeros((ck, ck), _F32))
    lax.fori_loop(0, nc - n_ctx_chunks, lambda i, s: bstep(nc - 1 - i, s), s)

    nw = nw_ref[...]

    def chunk_out(c, _):
        q = q_ref[rows(c), :]
        k = k_ref[rows(c), :]
        v = v_ref[rows(c), :]
        sc = _dot_nt(q, k) * dmask
        o = (_dot(sc.astype(_BF), v)
             + xi_f * _dot(q, kvf_ref[c].astype(_BF))
             + xi_b * _dot(q, kvb_ref[c].astype(_BF)))
        mu = jnp.mean(o, axis=-1, keepdims=True)
        d = o - mu
        var = jnp.mean(d * d, axis=-1, keepdims=True)
        on = d * lax.rsqrt(var + _EPS) * nw
        g = g_ref[rows(c), :].astype(_F32)
        o_ref[rows(c), :] = (on * (g * jax.nn.sigmoid(g))).astype(o_ref.dtype)
        return 0

    lax.fori_loop(0, nc, chunk_out, 0)


def _retention(p, lg, norm_w, C):
    B, S, _ = p.shape
    nc = S // 128
    blk = lambda cb: pl.BlockSpec((None, S, _HD), lambda b, h, cb=cb: (b, 0, cb + h))
    return pl.pallas_call(
        functools.partial(_retention_kernel, n_ctx_chunks=C // 128),
        grid=(B, _RET_H),
        in_specs=[pl.BlockSpec(memory_space=pltpu.SMEM),
                  blk(_CB_RET_Q), blk(_CB_RET_K), blk(_CB_RET_V), blk(_CB_RET_G),
                  pl.BlockSpec((None, 1, _HD), lambda b, h: (h, 0, 0))],
        out_specs=pl.BlockSpec((None, S, _HD), lambda b, h: (b, 0, h)),
        out_shape=jax.ShapeDtypeStruct((B, S, _RET_H * _HD), _BF),
        scratch_shapes=[pltpu.VMEM((nc, 128, 128), _F32), pltpu.VMEM((nc, 128, 128), _F32)],
        compiler_params=_params(("parallel", "parallel"), 48 * 2 ** 20),
        name="retention",
    )(lg, p, p, p, p, norm_w.reshape(_RET_H, 1, _HD))


def _swa_kernel(sink_ref, q0_ref, q1_ref, q2_ref, k_ref, v_ref, o_ref, *, n_ctx):
    hk = pl.program_id(1)
    S = k_ref.shape[0]
    C = n_ctx
    N = S - C
    w = _SWA_WIN
    q_refs = (q0_ref, q1_ref, q2_ref)
    kc = k_ref[0:C, :]
    vc = v_ref[0:C, :]

    def sink_col(rows_per_head):
        r = lax.broadcasted_iota(jnp.int32, (_SWA_GRP * rows_per_head, 1), 0)
        s = [sink_ref[hk * _SWA_GRP + g] * _LOG2E for g in range(_SWA_GRP)]
        return jnp.where(r < rows_per_head, s[0], jnp.where(r < 2 * rows_per_head, s[1], s[2]))

    def store(rows, o, n):
        for g in range(_SWA_GRP):
            o_ref[rows, g * _HD:(g + 1) * _HD] = o[g * n:(g + 1) * n].astype(o_ref.dtype)

    qc = jnp.concatenate([r[0:C, :] for r in q_refs], axis=0)
    s = _dot_nt(qc, kc)
    sk = sink_col(C)
    m = jnp.maximum(jnp.max(s, axis=-1, keepdims=True), sk)
    p = jnp.exp2(s - m)
    l = jnp.sum(p, axis=-1, keepdims=True) + jnp.exp2(sk - m)
    store(pl.ds(0, C), _dot(p.astype(_BF), vc) / l, C)

    sk_w = sink_col(w)
    rq = lax.broadcasted_iota(jnp.int32, (_SWA_GRP * w, 3 * w), 0) & (w - 1)
    ck = lax.broadcasted_iota(jnp.int32, (_SWA_GRP * w, 3 * w), 1)

    def block(i, _):
        q0 = pl.multiple_of(C + i * w, w)
        q = jnp.concatenate([r[pl.ds(q0, w), :] for r in q_refs], axis=0)
        ws = jnp.clip((i - 1) * w, 0, N - 3 * w)
        k0 = pl.multiple_of(C + ws, w)
        kw = k_ref[pl.ds(k0, 3 * w), :]
        vw = v_ref[pl.ds(k0, 3 * w), :]
        dist = (ws + ck) - (i * w + rq)
        sw = jnp.where(jnp.abs(dist) <= w, _dot_nt(q, kw), _NEG)
        sc = _dot_nt(q, kc)
        m = jnp.maximum(jnp.maximum(jnp.max(sw, axis=-1, keepdims=True),
                                    jnp.max(sc, axis=-1, keepdims=True)), sk_w)
        pw = jnp.exp2(sw - m)
        pc = jnp.exp2(sc - m)
        l = (jnp.sum(pw, axis=-1, keepdims=True) + jnp.sum(pc, axis=-1, keepdims=True)
             + jnp.exp2(sk_w - m))
        o = (_dot(pw.astype(_BF), vw) + _dot(pc.astype(_BF), vc)) / l
        store(pl.ds(q0, w), o, w)
        return 0

    lax.fori_loop(0, N // w, block, 0)


def _swa(p, sink, C):
    B, S, _ = p.shape
    qblk = lambda g: pl.BlockSpec((None, S, _HD), lambda b, hk, g=g: (b, 0, _CB_SWA_Q + hk * _SWA_GRP + g))
    return pl.pallas_call(
        functools.partial(_swa_kernel, n_ctx=C),
        grid=(B, _SWA_KV),
        in_specs=[pl.BlockSpec(memory_space=pltpu.SMEM), qblk(0), qblk(1), qblk(2),
                  pl.BlockSpec((None, S, _HD), lambda b, hk: (b, 0, _CB_SWA_K + hk)),
                  pl.BlockSpec((None, S, _HD), lambda b, hk: (b, 0, _CB_SWA_V + hk))],
        out_specs=pl.BlockSpec((None, S, _SWA_GRP * _HD), lambda b, hk: (b, 0, hk)),
        out_shape=jax.ShapeDtypeStruct((B, S, _SWA_Q * _HD), _BF),
        compiler_params=_params(("parallel", "parallel"), 48 * 2 ** 20),
        name="swa",
    )(sink, p, p, p, p, p)


def _diff_kernel(lam_ref, q_ref, k_ref, v_ref, nw_ref, o_ref, *, n_ctx, tk, out_scale):
    qi = pl.program_id(2)
    tq = q_ref.shape[0]
    S = k_ref.shape[0]
    q = q_ref[...]
    lane = lax.broadcasted_iota(jnp.int32, q.shape, 1)
    zero = jnp.zeros_like(q)
    qq = jnp.concatenate([jnp.where(lane < _DIFF_DQ, q, zero), jnp.where(lane >= _DIFF_DQ, q, zero)], axis=0)
    nblk = jnp.where(qi == 0, n_ctx // tk, S // tk)

    def body(kb, carry):
        m, l, acc = carry
        r = pl.ds(pl.multiple_of(kb * tk, tk), tk)
        s = _dot_nt(qq, k_ref[r, :])
        mn = jnp.maximum(m, jnp.max(s, axis=-1, keepdims=True))
        alpha = jnp.exp2(m - mn)
        p = jnp.exp2(s - mn)
        l = alpha * l + jnp.sum(p, axis=-1, keepdims=True)
        acc = alpha * acc + _dot(p.astype(_BF), v_ref[r, :])
        return mn, l, acc

    init = (jnp.full((2 * tq, 1), _NEG, _F32), jnp.zeros((2 * tq, 1), _F32), jnp.zeros((2 * tq, _HD), _F32))
    _, l, acc = lax.fori_loop(0, nblk, body, init)
    o = acc / l
    o = o[0:tq] - lam_ref[0] * o[tq:2 * tq]
    on = o * lax.rsqrt(jnp.mean(o * o, axis=-1, keepdims=True) + _EPS)
    o_ref[...] = (on * (nw_ref[...] * out_scale)).astype(o_ref.dtype)


def _diff(p, lam, norm_w, lambda_init, C):
    B, S, _ = p.shape
    tq = C
    return pl.pallas_call(
        functools.partial(_diff_kernel, n_ctx=C, tk=256, out_scale=1.0 - lambda_init),
        grid=(B, _DIFF_H, S // tq),
        in_specs=[pl.BlockSpec(memory_space=pltpu.SMEM),
                  pl.BlockSpec((None, tq, _HD), lambda b, h, i: (b, i, _CB_DIF_Q + h)),
                  pl.BlockSpec((None, S, _HD), lambda b, h, i: (b, 0, _CB_DIF_K + h)),
                  pl.BlockSpec((None, S, _HD), lambda b, h, i: (b, 0, _CB_DIF_V + h)),
                  pl.BlockSpec((1, _HD), lambda b, h, i: (0, 0))],
        out_specs=pl.BlockSpec((None, tq, _HD), lambda b, h, i: (b, i, h)),
        out_shape=jax.ShapeDtypeStruct((B, S, _DIFF_H * _HD), _BF),
        compiler_params=_params(("parallel", "parallel", "arbitrary"), 40 * 2 ** 20),
        name="diff_attn",
    )(lam, p, p, p, norm_w.reshape(1, _HD))


def _outproj_kernel(r_ref, s_ref, d_ref, wr_ref, ws_ref, wd_ref, o_ref):
    o_ref[...] = (_dot(r_ref[...], wr_ref[...]) + _dot(s_ref[...], ws_ref[...])
                  + _dot(d_ref[...], wd_ref[...]))


def _outproj(ret_y, swa_y, dif_y, w_bf):
    B, S, _ = ret_y.shape
    D = w_bf.shape[1]
    tm = 768 if S % 768 == 0 else 256
    tn = min(512, D)
    wr, wsw, wd = ret_y.shape[2], swa_y.shape[2], dif_y.shape[2]
    w_r, w_s, w_d = w_bf[0:wr], w_bf[wr:wr + wsw], w_bf[wr + wsw:]
    act = lambda width: pl.BlockSpec((None, tm, width), lambda b, i, j: (b, i, 0))
    wgt = lambda width: pl.BlockSpec((width, tn), lambda b, i, j: (0, j))
    return pl.pallas_call(
        _outproj_kernel,
        grid=(B, S // tm, D // tn),
        in_specs=[act(wr), act(wsw), act(wd), wgt(wr), wgt(wsw), wgt(wd)],
        out_specs=pl.BlockSpec((None, tm, tn), lambda b, i, j: (b, i, j)),
        out_shape=jax.ShapeDtypeStruct((B, S, D), _F32),
        compiler_params=_params(("parallel", "parallel", "arbitrary"), 48 * 2 ** 20),
        name="out_proj",
    )(ret_y, swa_y, dif_y, w_r, w_s, w_d)


def _ln_core(z, g, b):
    mu = jnp.mean(z, axis=-1, keepdims=True)
    d = z - mu
    var = jnp.mean(d * d, axis=-1, keepdims=True)
    return d * lax.rsqrt(var + _EPS) * g + b


def _norm_mod_kernel(x_ref, y_ref, gate_ref, lng_ref, lnb_ref, sc_ref, sh_ref, xo_ref, ho_ref, *, alpha):
    xn = _ln_core(alpha * x_ref[...] + gate_ref[...] * y_ref[...], lng_ref[...], lnb_ref[...])
    xo_ref[...] = xn
    ho_ref[...] = (xn * (1.0 + sc_ref[...]) + sh_ref[...]).astype(ho_ref.dtype)


def _norm_mod(x, y, modv, l_gate, gate_chunk, l_next, lng, lnb, alpha, h_dtype):
    B, S, D = x.shape
    tr = _ROW_TILE
    row = pl.BlockSpec((None, tr, D), lambda b, t: (b, t, 0))
    vec = lambda l, chunk: pl.BlockSpec((None, 1, D), _mod_index(l, chunk))
    par = pl.BlockSpec((1, D), lambda b, t: (0, 0))
    l_n, sh_chunk, sc_chunk = l_next
    return pl.pallas_call(
        functools.partial(_norm_mod_kernel, alpha=alpha),
        grid=(B, S // tr),
        in_specs=[row, row, vec(l_gate, gate_chunk), par, par, vec(l_n, sc_chunk), vec(l_n, sh_chunk)],
        out_specs=[row, row],
        out_shape=[jax.ShapeDtypeStruct((B, S, D), _F32), jax.ShapeDtypeStruct((B, S, D), h_dtype)],
        compiler_params=_params(("parallel", "parallel"), 48 * 2 ** 20),
        name="norm_mod",
    )(x, y, modv, lng.reshape(1, D), lnb.reshape(1, D), modv, modv)


def _split_bf(x):
    hi = x.astype(_BF)
    lo = (x - hi.astype(_F32)).astype(_BF)
    return hi, lo


def _router_kernel(h_ref, wt_ref, b_ref, idx_ref, rank_ref, gw_ref, cnt_ref, carry_ref):
    t = pl.program_id(0)
    tr = h_ref.shape[0]

    @pl.when(t == 0)
    def _():
        carry_ref[...] = jnp.zeros_like(carry_ref)

    h_hi, h_lo = _split_bf(h_ref[...])
    w_hi, w_lo = _split_bf(wt_ref[...])
    logits = _dot_nt(w_hi, h_hi) + _dot_nt(w_hi, h_lo) + _dot_nt(w_lo, h_hi) + b_ref[...]
    e_iota = lax.broadcasted_iota(jnp.int32, logits.shape, 0)
    tops, sels = [], []
    cur = logits
    for _ in range(_TOP_K):
        m = jnp.max(cur, axis=0, keepdims=True)
        first = jnp.min(jnp.where(cur == m, e_iota, _N_EXP), axis=0, keepdims=True)
        sel = e_iota == first
        tops.append(m)
        sels.append(sel)
        idx_ref[len(tops) - 1:len(tops), :] = first
        cur = jnp.where(sel, -jnp.inf, cur)
    ex = [jnp.exp(v - tops[0]) for v in tops]
    den = ex[0] + ex[1] + ex[2] + ex[3]
    for k in range(_TOP_K):
        gw_ref[k:k + 1, :] = ex[k] / den
    member = jnp.zeros(logits.shape, _F32)
    for sel in sels:
        member = member + jnp.where(sel, 1.0, 0.0)
    ri = lax.broadcasted_iota(jnp.int32, (tr, tr), 0)
    ci = lax.broadcasted_iota(jnp.int32, (tr, tr), 1)
    upper = jnp.where(ri < ci, 1.0, 0.0).astype(_BF)
    before = _dot(member.astype(_BF), upper) + carry_ref[:, 0:1]
    for k in range(_TOP_K):
        rank_ref[k:k + 1, :] = jnp.sum(jnp.where(sels[k], before, 0.0), axis=0, keepdims=True).astype(jnp.int32)
    carry_ref[...] = carry_ref[...] + jnp.sum(member, axis=1, keepdims=True)
    cnt_ref[...] = carry_ref[...]


def _router(h2d, w_router, b_router):
    T, D = h2d.shape
    tr = 256
    k_out = lambda dt: jax.ShapeDtypeStruct((_TOP_K, T), dt)
    k_spec = pl.BlockSpec((_TOP_K, tr), lambda t: (0, t))
    return pl.pallas_call(
        _router_kernel,
        grid=(T // tr,),
        in_specs=[pl.BlockSpec((tr, D), lambda t: (t, 0)),
                  pl.BlockSpec((_N_EXP, D), lambda t: (0, 0)),
                  pl.BlockSpec((_N_EXP, 1), lambda t: (0, 0))],
        out_specs=[k_spec, k_spec, k_spec, pl.BlockSpec((_N_EXP, 128), lambda t: (0, 0))],
        out_shape=[k_out(jnp.int32), k_out(jnp.int32), k_out(_F32),
                   jax.ShapeDtypeStruct((_N_EXP, 128), _F32)],
        scratch_shapes=[pltpu.VMEM((_N_EXP, 128), _F32)],
        compiler_params=_params(("arbitrary",), 32 * 2 ** 20),
        name="moe_router",
    )(h2d, w_router.T, b_router.reshape(_N_EXP, 1))


def _gather_kernel(src_ref, x_hbm, o_ref, sem):
    n = o_ref.shape[0]

    def issue(r, _):
        pltpu.make_async_copy(x_hbm.at[pl.ds(src_ref[0, r], 1)], o_ref.at[pl.ds(r, 1)], sem).start()
        return 0

    lax.fori_loop(0, n, issue, 0)

    def wait(r, _):
        pltpu.make_async_copy(x_hbm.at[pl.ds(0, 1)], o_ref.at[pl.ds(r, 1)], sem).wait()
        return 0

    lax.fori_loop(0, n, wait, 0)


def _gather_rows(x2d, src):
    P = src.shape[0]
    D = x2d.shape[1]
    tg = _MOE_TM
    return pl.pallas_call(
        _gather_kernel,
        grid=(P // tg,),
        in_specs=[pl.BlockSpec((None, 1, tg), lambda i: (i, 0, 0), memory_space=pltpu.SMEM),
                  pl.BlockSpec(memory_space=pl.ANY)],
        out_specs=pl.BlockSpec((tg, D), lambda i: (i, 0)),
        out_shape=jax.ShapeDtypeStruct((P, D), x2d.dtype),
        scratch_shapes=[pltpu.SemaphoreType.DMA(())],
        compiler_params=_params(("arbitrary",), 32 * 2 ** 20),
        name="moe_gather",
    )(src.reshape(P // tg, 1, tg), x2d)


def _gate_up_kernel(te_ref, na_ref, x_ref, w_ref, b_ref, o_ref):
    i = pl.program_id(0)
    f = o_ref.shape[1]

    @pl.when(i < na_ref[0])
    def _():
        gu = _dot(x_ref[...].astype(_BF), w_ref[...]) + b_ref[...]
        gate = jnp.minimum(gu[:, 0:f], _LIMIT)
        up = jnp.clip(gu[:, f:2 * f], -_LIMIT, _LIMIT)
        o_ref[...] = ((up + 1.0) * gate * jax.nn.sigmoid(_SWIGLU_ALPHA * gate)).astype(o_ref.dtype)

    @pl.when(i >= na_ref[0])
    def _():
        o_ref[...] = jnp.zeros_like(o_ref)


def _down_kernel(te_ref, na_ref, a_ref, w_ref, b_ref, o_ref):
    i = pl.program_id(0)

    @pl.when(i < na_ref[0])
    def _():
        o_ref[...] = _dot(a_ref[...], w_ref[...]) + b_ref[...]

    @pl.when(i >= na_ref[0])
    def _():
        o_ref[...] = jnp.zeros_like(o_ref)


def _expert_ffn(xs, tile_expert, n_active, w_gu, b_gu, w_dn, b_dn):
    P, D = xs.shape
    E, _, F2 = w_gu.shape
    F = F2 // 2
    tm = _MOE_TM
    nt = P // tm
    row = lambda width: pl.BlockSpec((tm, width), lambda i, te, na: (jnp.minimum(i, na[0] - 1), 0))
    act = pl.pallas_call(
        _gate_up_kernel,
        grid_spec=pltpu.PrefetchScalarGridSpec(
            num_scalar_prefetch=2, grid=(nt,),
            in_specs=[row(D),
                      pl.BlockSpec((None, D, F2), lambda i, te, na: (te[i], 0, 0)),
                      pl.BlockSpec((None, 1, F2), lambda i, te, na: (te[i], 0, 0))],
            out_specs=pl.BlockSpec((tm, F), lambda i, te, na: (i, 0))),
        out_shape=jax.ShapeDtypeStruct((P, F), _BF),
        compiler_params=_params(("arbitrary",), _VMEM_CAP),
        name="moe_gate_up",
    )(tile_expert, n_active, xs, w_gu, b_gu.reshape(E, 1, F2))
    return pl.pallas_call(
        _down_kernel,
        grid_spec=pltpu.PrefetchScalarGridSpec(
            num_scalar_prefetch=2, grid=(nt,),
            in_specs=[row(F),
                      pl.BlockSpec((None, F, D), lambda i, te, na: (te[i], 0, 0)),
                      pl.BlockSpec((None, 1, D), lambda i, te, na: (te[i], 0, 0))],
            out_specs=pl.BlockSpec((tm, D), lambda i, te, na: (i, 0))),
        out_shape=jax.ShapeDtypeStruct((P, D), _F32),
        compiler_params=_params(("arbitrary",), _VMEM_CAP),
        name="moe_down",
    )(tile_expert, n_active, act, w_dn, b_dn.reshape(E, 1, D))


def _combine_kernel(dest_ref, y_hbm, gw_ref, x_ref, gate_ref, lng_ref, lnb_ref, sc_ref, sh_ref,
                    xo_ref, ho_ref, buf_ref, sem, *, alpha):
    tc = x_ref.shape[0]

    def issue(r, _):
        for k in range(_TOP_K):
            pltpu.make_async_copy(y_hbm.at[pl.ds(dest_ref[0, k * tc + r], 1)],
                                  buf_ref.at[k, pl.ds(r, 1)], sem).start()
        return 0

    lax.fori_loop(0, tc, issue, 0)

    def wait(r, _):
        for k in range(_TOP_K):
            pltpu.make_async_copy(y_hbm.at[pl.ds(0, 1)], buf_ref.at[k, pl.ds(r, 1)], sem).wait()
        return 0

    lax.fori_loop(0, tc, wait, 0)
    gw = gw_ref[...]
    f = gw[:, 0:1] * buf_ref[0]
    for k in range(1, _TOP_K):
        f = f + gw[:, k:k + 1] * buf_ref[k]
    xn = _ln_core(alpha * x_ref[...] + gate_ref[...] * f, lng_ref[...], lnb_ref[...])
    xo_ref[...] = xn
    ho_ref[...] = (xn * (1.0 + sc_ref[...]) + sh_ref[...]).astype(ho_ref.dtype)


def _combine_norm(ys, dest, gw, x, modv, l, l_next, lng, lnb, alpha, n_ctx):
    B, S, D = x.shape
    tc = 128
    tpb = S // tc
    T = B * S
    dest_t = dest.reshape(_TOP_K, T // tc, tc).transpose(1, 0, 2).reshape(T // tc, 1, _TOP_K * tc)
    ctx_tiles = n_ctx // tc

    def mod_index(l_, chunk):
        def index(b, t):
            return ((l_ * 8 + jnp.where(t < ctx_tiles, 2, b)) * 6 + chunk, 0, 0)
        return index

    row = pl.BlockSpec((None, tc, D), lambda b, t: (b, t, 0))
    vec = lambda l_, chunk: pl.BlockSpec((None, 1, D), mod_index(l_, chunk))
    par = pl.BlockSpec((1, D), lambda b, t: (0, 0))
    l_n, sh_chunk, sc_chunk = l_next
    return pl.pallas_call(
        functools.partial(_combine_kernel, alpha=alpha),
        grid=(B, tpb),
        in_specs=[pl.BlockSpec((None, 1, _TOP_K * tc), lambda b, t: (b * tpb + t, 0, 0), memory_space=pltpu.SMEM),
                  pl.BlockSpec(memory_space=pl.ANY),
                  pl.BlockSpec((tc, _TOP_K), lambda b, t: (b * tpb + t, 0)),
                  row, vec(l, 5), par, par, vec(l_n, sc_chunk), vec(l_n, sh_chunk)],
        out_specs=[row, row],
        out_shape=[jax.ShapeDtypeStruct((B, S, D), _F32), jax.ShapeDtypeStruct((B, S, D), _BF)],
        scratch_shapes=[pltpu.VMEM((_TOP_K, tc, D), _F32), pltpu.SemaphoreType.DMA(())],
        compiler_params=_params(("arbitrary", "arbitrary"), 48 * 2 ** 20),
        name="moe_combine_norm",
    )(dest_t, ys, gw, x, modv, lng.reshape(1, D), lnb.reshape(1, D), modv, modv)


def _moe_plan(idx, rank, counts, T):
    tm = _MOE_TM
    nt = -(-_TOP_K * T // tm) + _N_EXP
    cnt = counts[:, 0].astype(jnp.int32)
    padded = (cnt + tm - 1) // tm * tm
    ends = jnp.cumsum(padded)
    starts = ends - padded
    dest = starts[idx] + rank
    n_active = (ends[-1] // tm).astype(jnp.int32)
    tile_start = jnp.arange(nt, dtype=jnp.int32) * tm
    te = jnp.searchsorted(ends, tile_start, side="right").astype(jnp.int32)
    last = jnp.searchsorted(ends, (n_active - 1) * tm, side="right").astype(jnp.int32)
    te = jnp.minimum(jnp.where(tile_start < ends[-1], te, last), _N_EXP - 1)
    tok = jnp.broadcast_to(jnp.arange(T, dtype=jnp.int32)[None, :], dest.shape)
    src = jnp.zeros((nt * tm,), jnp.int32).at[dest.reshape(-1)].set(tok.reshape(-1))
    return dest, src, te, n_active.reshape(1)


def kernel(x, c, ctx, c_ctx, w_ada, b_ada, w_in, w_out, ret_decay_f, ret_decay_b, ret_norm_w, swa_sink,
           diff_lambda, diff_norm_w, ln1_g, ln1_b, ln2_g, ln2_b, w_router, b_router, w_gate_up, b_gate_up,
           w_down, b_down):
    B, N, D = x.shape
    C = ctx.shape[1]
    S = C + N
    L = w_in.shape[0]
    T = B * S
    alpha = (2 * L) ** 0.25
    u = jnp.concatenate([ctx, x], axis=1)
    modv = _ada_mods(c, c_ctx, w_ada, b_ada)
    tabs = _rope_tables(S, C)
    h = _modulate(u, modv, 0)
    for l in range(L):
        lambda_init = 0.8 - 0.6 * math.exp(-0.3 * l)
        p = _inproj(h, w_in[l].astype(_BF), tabs, C)
        lg = jnp.stack([jax.nn.log_sigmoid(ret_decay_f[l].astype(_F32)),
                        jax.nn.log_sigmoid(ret_decay_b[l].astype(_F32))])
        lp = diff_lambda[l].astype(_F32)
        lam = (jnp.exp(jnp.sum(lp[0] * lp[1])) - jnp.exp(jnp.sum(lp[2] * lp[3])) + lambda_init).reshape(1)
        ret_y = _retention(p, lg, ret_norm_w[l], C)
        swa_y = _swa(p, swa_sink[l].astype(_F32), C)
        dif_y = _diff(p, lam, diff_norm_w[l], lambda_init, C)
        y = _outproj(ret_y, swa_y, dif_y, w_out[l].astype(_BF))
        u, h2 = _norm_mod(u, y, modv, l, 2, (l, 3, 4), ln1_g[l], ln1_b[l], alpha, _F32)
        h2d = h2.reshape(T, D)
        idx, rank, gw, counts = _router(h2d, w_router[l], b_router[l])
        dest, src, te, n_active = _moe_plan(idx, rank, counts, T)
        xs = _gather_rows(h2d, src)
        F = w_gate_up.shape[-1] // 2
        w_gu = jnp.concatenate([w_gate_up[l][..., 0::2], w_gate_up[l][..., 1::2]], axis=-1).astype(_BF)
        b_gu = jnp.concatenate([b_gate_up[l][..., 0::2], b_gate_up[l][..., 1::2]], axis=-1)
        ys = _expert_ffn(xs, te, n_active, w_gu, b_gu, w_down[l].astype(_BF), b_down[l])
        l_next = (min(l + 1, L - 1), 0, 1)
        u, h = _combine_norm(ys, dest, gw.T, u, modv, l, l_next, ln2_g[l], ln2_b[l], alpha, C)
    return u[:, C:, :]
```

```python
import functools
import math

import numpy as np
import jax
import jax.numpy as jnp
from jax import lax
from jax.experimental import pallas as pl
from jax.experimental.pallas import tpu as pltpu

_F32 = jnp.float32
_BF = jnp.bfloat16

_GRID_W = 64
_HD = 128
_RET_H = 8
_SWA_Q = 12
_SWA_KV = 4
_SWA_GRP = _SWA_Q // _SWA_KV
_SWA_WIN = 128
_DIFF_H = 12
_DIFF_DQ = 64
_N_EXP = 32
_TOP_K = 4
_LIMIT = 7.0
_SWIGLU_ALPHA = 1.702
_ROPE_THETA = 10000.0
_EPS = 1e-5
_NEG = -1e30
_LOG2E = 1.4426950408889634

_CB_RET_Q, _CB_RET_K, _CB_RET_V, _CB_RET_G = 0, 8, 16, 24
_CB_SWA_Q, _CB_SWA_K, _CB_SWA_V = 32, 44, 48
_CB_DIF_Q, _CB_DIF_K, _CB_DIF_V = 52, 64, 76
_IN_W = 88 * 128

_ROW_TILE = 256
_MOE_TM = 256
_VMEM_CAP = 56 * 2 ** 20


def _params(sem, vmem=None):
    return pltpu.CompilerParams(dimension_semantics=sem, vmem_limit_bytes=vmem)


def _dot(a, b):
    return jnp.dot(a, b, preferred_element_type=_F32)


def _dot_nt(a, b):
    return lax.dot_general(a, b, (((1,), (1,)), ((), ())), preferred_element_type=_F32)


def _dot_tn(a, b):
    return lax.dot_general(a, b, (((0,), (0,)), ((), ())), preferred_element_type=_F32)


def _ada_kernel(ct_ref, w_ref, b_ref, o_ref):
    c = ct_ref[...]
    c = c * jax.nn.sigmoid(c)
    w = w_ref[...]
    rows = [jnp.sum(w * c[:, r:r + 1], axis=0, keepdims=True) for r in range(3)]
    rows.append(jnp.zeros((5, w.shape[1]), _F32))
    o_ref[...] = jnp.concatenate(rows, axis=0) + b_ref[...]


def _ada_mods(c, c_ctx, w_ada, b_ada):
    L, D, W6 = w_ada.shape
    ct = jnp.zeros((D, 8), _F32).at[:, 0:2].set(c.T).at[:, 2].set(c_ctx)
    tn = 512
    out = pl.pallas_call(
        _ada_kernel,
        grid=(L, W6 // tn),
        in_specs=[pl.BlockSpec((D, 8), lambda l, j: (0, 0)),
                  pl.BlockSpec((None, D, tn), lambda l, j: (l, 0, j)),
                  pl.BlockSpec((None, 1, tn), lambda l, j: (l, 0, j))],
        out_specs=pl.BlockSpec((None, 8, tn), lambda l, j: (l, 0, j)),
        out_shape=jax.ShapeDtypeStruct((L, 8, W6), _F32),
        compiler_params=_params(("arbitrary", "arbitrary"), 40 * 2 ** 20),
        name="ada_mods",
    )(ct, w_ada, b_ada.reshape(L, 1, W6))
    return out.reshape(L * 8 * 6, 1, D)


def _mod_index(l, chunk):
    def index(b, t, *_):
        return ((l * 8 + jnp.where(t == 0, 2, b)) * 6 + chunk, 0, 0)
    return index


def _modulate_kernel(x_ref, sc_ref, sh_ref, h_ref):
    h_ref[...] = (x_ref[...] * (1.0 + sc_ref[...]) + sh_ref[...]).astype(h_ref.dtype)


def _modulate(u, modv, l):
    B, S, D = u.shape
    tr = _ROW_TILE
    vec = lambda chunk: pl.BlockSpec((None, 1, D), _mod_index(l, chunk))
    return pl.pallas_call(
        _modulate_kernel,
        grid=(B, S // tr),
        in_specs=[pl.BlockSpec((None, tr, D), lambda b, t: (b, t, 0)), vec(1), vec(0)],
        out_specs=pl.BlockSpec((None, tr, D), lambda b, t: (b, t, 0)),
        out_shape=jax.ShapeDtypeStruct((B, S, D), _BF),
        compiler_params=_params(("arbitrary", "arbitrary"),40 * 2 ** 20),
        name="modulate",
    )(u, modv, modv)


def _rope_tables(S, C):
    n = S - C
    t = np.arange(n)
    row, col = t // _GRID_W, t % _GRID_W
    out_cos, out_sa, out_sb = [], [], []
    for seg in (64, 32):
        half = seg // 2
        lane = np.arange(128)
        j = lane % seg
        pos = np.where((lane // seg) % 2 == 0, row[:, None], col[:, None]).astype(np.float32)
        freqs = (_ROPE_THETA ** (-(np.arange(half, dtype=np.float32)) / half)).astype(np.float32)
        ang = pos * freqs[j % half][None, :]
        cos, sin = np.cos(ang), np.sin(ang)
        first = (j < half)[None, :]
        ident = np.zeros((C, 128), np.float32)
        out_cos.append(np.concatenate([ident + 1.0, cos], 0))
        out_sa.append(np.concatenate([ident, np.where(first, -sin, 0.0)], 0))
        out_sb.append(np.concatenate([ident, np.where(first, 0.0, sin)], 0))
    mk = lambda parts: jnp.asarray(np.stack(parts).astype(np.float32))
    return mk(out_cos), mk(out_sa), mk(out_sb)


def _inproj_kernel(kind_ref, a_ref, w_ref, cos_ref, sa_ref, sb_ref, scale_ref, o_ref, acc_ref):
    j = pl.program_id(2)
    acc_ref[...] = _dot(a_ref[...], w_ref[...])
    kind = kind_ref[j]

    @pl.when(kind == 0)
    def _():
        o_ref[...] = acc_ref[...].astype(o_ref.dtype)

    def rotate(half):
        s = scale_ref[j]
        cos, sa, sb = cos_ref[...] * s, sa_ref[...] * s, sb_ref[...] * s
        for cc in range(o_ref.shape[1] // 128):
            x = acc_ref[:, cc * 128:(cc + 1) * 128]
            y = x * cos + pltpu.roll(x, 128 - half, 1) * sa + pltpu.roll(x, half, 1) * sb
            o_ref[:, cc * 128:(cc + 1) * 128] = y.astype(o_ref.dtype)

    @pl.when(kind == 1)
    def _():
        rotate(32)

    @pl.when(kind == 2)
    def _():
        rotate(16)


def _inproj(h, w_bf, tabs, C):
    B, S, D = h.shape
    tn = 512
    tm = 768 if S % 768 == 0 else 256
    nj = _IN_W // tn
    kinds, scales = np.zeros((nj,), np.int32), np.ones((nj,), np.float32)
    for j in range(nj):
        cb = j * tn // 128
        if _CB_SWA_Q <= cb < _CB_SWA_V:
            kinds[j] = 1
            if cb < _CB_SWA_K:
                scales[j] = _HD ** -0.5 * _LOG2E
        elif _CB_DIF_Q <= cb < _CB_DIF_V:
            kinds[j] = 2
            if cb < _CB_DIF_K:
                scales[j] = _DIFF_DQ ** -0.5 * _LOG2E
    tabsel = lambda b, i, j, kind: (jnp.maximum(kind[j] - 1, 0), i, 0)
    cos, sa, sb = tabs
    grid_spec = pltpu.PrefetchScalarGridSpec(
        num_scalar_prefetch=1,
        grid=(B, S // tm, nj),
        in_specs=[pl.BlockSpec((None, tm, D), lambda b, i, j, kind: (b, i, 0)),
                  pl.BlockSpec((D, tn), lambda b, i, j, kind: (0, j)),
                  pl.BlockSpec((None, tm, 128), tabsel),
                  pl.BlockSpec((None, tm, 128), tabsel),
                  pl.BlockSpec((None, tm, 128), tabsel),
                  pl.BlockSpec(memory_space=pltpu.SMEM)],
        out_specs=pl.BlockSpec((None, tm, tn), lambda b, i, j, kind: (b, i, j)),
        scratch_shapes=[pltpu.VMEM((tm, tn), _F32)])
    return pl.pallas_call(
        _inproj_kernel,
        grid_spec=grid_spec,
        out_shape=jax.ShapeDtypeStruct((B, S, _IN_W), _BF),
        compiler_params=_params(("arbitrary", "arbitrary", "arbitrary"),48 * 2 ** 20),
        name="in_proj",
    )(jnp.asarray(kinds), h, w_bf, cos, sa, sb, jnp.asarray(scales))


def _retention_kernel(lg_ref, q_ref, k_ref, v_ref, g_ref, nw_ref, o_ref, kvf_ref, kvb_ref, *, n_ctx_chunks):
    h = pl.program_id(1)
    nc = kvf_ref.shape[0]
    ck = 128
    lgf, lgb = lg_ref[0, h], lg_ref[1, h]
    sck = _HD ** -0.5
    ii = lax.broadcasted_iota(jnp.int32, (ck, ck), 0).astype(_F32)
    jj = lax.broadcasted_iota(jnp.int32, (ck, ck), 1).astype(_F32)
    rel = ii - jj
    dmask = jnp.where(rel >= 0.0, jnp.exp(lgf * jnp.maximum(rel, 0.0)),
                      jnp.exp(lgb * jnp.maximum(-rel, 0.0))) * sck
    col = lax.broadcasted_iota(jnp.int32, (ck, 1), 0).astype(_F32)
    zeta_f = jnp.exp(lgf * (ck - 1.0 - col)) * sck
    zeta_b = jnp.exp(lgb * col) * sck
    xi_f = jnp.exp(lgf * (col + 1.0))
    xi_b = jnp.exp(lgb * (ck - col))
    g_f = jnp.exp(lgf * ck)
    g_b = jnp.exp(lgb * ck)

    def rows(c):
        return pl.ds(pl.multiple_of(c * ck, ck), ck)

    def chunk_kv(c, _):
        k = k_ref[rows(c), :].astype(_F32)
        v = v_ref[rows(c), :]
        kvf_ref[c] = _dot_tn((k * zeta_f).astype(_BF), v)
        kvb_ref[c] = _dot_tn((k * zeta_b).astype(_BF), v)
        return 0

    lax.fori_loop(0, nc, chunk_kv, 0)

    def fscan(c, s):
        t = kvf_ref[c]
        kvf_ref[c] = s
        return g_f * s + t

    lax.fori_loop(0, nc, fscan, jnp.zeros((ck, ck), _F32))

    def bstep(c, s):
        t = kvb_ref[c]
        kvb_ref[c] = s
        return g_b * s + t

    s = lax.fori_loop(0, n_ctx_chunks, lambda i, s: bstep(n_ctx_chunks - 1 - i, s), jnp.zeros((ck, ck), _F32))
    lax.fori_loop(0, nc - n_ctx_chunks, lambda i, s: bstep(nc - 1 - i, s), s)

    nw = nw_ref[...]

    def chunk_out(c, _):
        q = q_ref[rows(c), :]
        k = k_ref[rows(c), :]
        v = v_ref[rows(c), :]
        sc = _dot_nt(q, k) * dmask
        o = (_dot(sc.astype(_BF), v)
             + xi_f * _dot(q, kvf_ref[c].astype(_BF))
             + xi_b * _dot(q, kvb_ref[c].astype(_BF)))
        mu = jnp.mean(o, axis=-1, keepdims=True)
        d = o - mu
        var = jnp.mean(d * d, axis=-1, keepdims=True)
        on = d * lax.rsqrt(var + _EPS) * nw
        g = g_ref[rows(c), :].astype(_F32)
        o_ref[rows(c), :] = (on * (g * jax.nn.sigmoid(g))).astype(o_ref.dtype)
        return 0

    lax.fori_loop(0, nc, chunk_out, 0)


def _retention(p, lg, norm_w, C):
    B, S, _ = p.shape
    nc = S // 128
    blk = lambda cb: pl.BlockSpec((None, S, _HD), lambda b, h, cb=cb: (b, 0, cb + h))
    return pl.pallas_call(
        functools.partial(_retention_kernel, n_ctx_chunks=C // 128),
        grid=(B, _RET_H),
        in_specs=[pl.BlockSpec(memory_space=pltpu.SMEM),
                  blk(_CB_RET_Q), blk(_CB_RET_K), blk(_CB_RET_V), blk(_CB_RET_G),
                  pl.BlockSpec((None, 1, _HD), lambda b, h: (h, 0, 0))],
        out_specs=pl.BlockSpec((None, S, _HD), lambda b, h: (b, 0, h)),
        out_shape=jax.ShapeDtypeStruct((B, S, _RET_H * _HD), _BF),
        scratch_shapes=[pltpu.VMEM((nc, 128, 128), _F32), pltpu.VMEM((nc, 128, 128), _F32)],
        compiler_params=_params(("arbitrary", "arbitrary"),48 * 2 ** 20),
        name="retention",
    )(lg, p, p, p, p, norm_w.reshape(_RET_H, 1, _HD))


def _swa_kernel(sink_ref, q0_ref, q1_ref, q2_ref, k_ref, v_ref, o_ref, *, n_ctx):
    hk = pl.program_id(1)
    S = k_ref.shape[0]
    C = n_ctx
    N = S - C
    w = _SWA_WIN
    q_refs = (q0_ref, q1_ref, q2_ref)
    kc = k_ref[0:C, :]
    vc = v_ref[0:C, :]

    def sink_col(rows_per_head):
        r = lax.broadcasted_iota(jnp.int32, (_SWA_GRP * rows_per_head, 1), 0)
        s = [sink_ref[hk * _SWA_GRP + g] * _LOG2E for g in range(_SWA_GRP)]
        return jnp.where(r < rows_per_head, s[0], jnp.where(r < 2 * rows_per_head, s[1], s[2]))

    def store(rows, o, n):
        for g in range(_SWA_GRP):
            o_ref[rows, g * _HD:(g + 1) * _HD] = o[g * n:(g + 1) * n].astype(o_ref.dtype)

    qc = jnp.concatenate([r[0:C, :] for r in q_refs], axis=0)
    s = _dot_nt(qc, kc)
    sk = sink_col(C)
    m = jnp.maximum(jnp.max(s, axis=-1, keepdims=True), sk)
    p = jnp.exp2(s - m)
    l = jnp.sum(p, axis=-1, keepdims=True) + jnp.exp2(sk - m)
    store(pl.ds(0, C), _dot(p.astype(_BF), vc) / l, C)

    sk_w = sink_col(w)
    rq = lax.broadcasted_iota(jnp.int32, (_SWA_GRP * w, 3 * w), 0) & (w - 1)
    ck = lax.broadcasted_iota(jnp.int32, (_SWA_GRP * w, 3 * w), 1)

    def block(i, _):
        q0 = pl.multiple_of(C + i * w, w)
        q = jnp.concatenate([r[pl.ds(q0, w), :] for r in q_refs], axis=0)
        ws = jnp.clip((i - 1) * w, 0, N - 3 * w)
        k0 = pl.multiple_of(C + ws, w)
        kw = k_ref[pl.ds(k0, 3 * w), :]
        vw = v_ref[pl.ds(k0, 3 * w), :]
        dist = (ws + ck) - (i * w + rq)
        sw = jnp.where(jnp.abs(dist) <= w, _dot_nt(q, kw), _NEG)
        sc = _dot_nt(q, kc)
        m = jnp.maximum(jnp.maximum(jnp.max(sw, axis=-1, keepdims=True),
                                    jnp.max(sc, axis=-1, keepdims=True)), sk_w)
        pw = jnp.exp2(sw - m)
        pc = jnp.exp2(sc - m)
        l = (jnp.sum(pw, axis=-1, keepdims=True) + jnp.sum(pc, axis=-1, keepdims=True)
             + jnp.exp2(sk_w - m))
        o = (_dot(pw.astype(_BF), vw) + _dot(pc.astype(_BF), vc)) / l
        store(pl.ds(q0, w), o, w)
        return 0

    lax.fori_loop(0, N // w, block, 0)


def _swa(p, sink, C):
    B, S, _ = p.shape
    qblk = lambda g: pl.BlockSpec((None, S, _HD), lambda b, hk, g=g: (b, 0, _CB_SWA_Q + hk * _SWA_GRP + g))
    return pl.pallas_call(
        functools.partial(_swa_kernel, n_ctx=C),
        grid=(B, _SWA_KV),
        in_specs=[pl.BlockSpec(memory_space=pltpu.SMEM), qblk(0), qblk(1), qblk(2),
                  pl.BlockSpec((None, S, _HD), lambda b, hk: (b, 0, _CB_SWA_K + hk)),
                  pl.BlockSpec((None, S, _HD), lambda b, hk: (b, 0, _CB_SWA_V + hk))],
        out_specs=pl.BlockSpec((None, S, _SWA_GRP * _HD), lambda b, hk: (b, 0, hk)),
        out_shape=jax.ShapeDtypeStruct((B, S, _SWA_Q * _HD), _BF),
        compiler_params=_params(("arbitrary", "arbitrary"),48 * 2 ** 20),
        name="swa",
    )(sink, p, p, p, p, p)


def _diff_ctx_tile(lam_ref, q_ref, k_ref, v_ref, nw_ref, o_ref, *, n_ctx, out_scale):
    tq = q_ref.shape[0]
    q = q_ref[...]
    lane = lax.broadcasted_iota(jnp.int32, q.shape, 1)
    zero = jnp.zeros_like(q)
    qq = jnp.concatenate([jnp.where(lane < _DIFF_DQ, q, zero), jnp.where(lane >= _DIFF_DQ, q, zero)], axis=0)
    s = _dot_nt(qq, k_ref[0:n_ctx, :])
    p = jnp.exp2(s - jnp.max(s, axis=-1, keepdims=True))
    o = _dot(p.astype(_BF), v_ref[0:n_ctx, :]) / jnp.sum(p, axis=-1, keepdims=True)
    o = o[0:tq] - lam_ref[0] * o[tq:2 * tq]
    on = o * lax.rsqrt(jnp.mean(o * o, axis=-1, keepdims=True) + _EPS)
    o_ref[...] = (on * (nw_ref[...] * out_scale)).astype(o_ref.dtype)


def _diff_lat_tile(lam_ref, q_ref, k_ref, vt_ref, nw_ref, o_ref, s_ref, p_ref, *, tk, out_scale):
    tq = q_ref.shape[0]
    S = k_ref.shape[0]
    nb = S // tk
    w2 = 2 * tq
    qt = q_ref[...].astype(_F32).T
    feat = lax.broadcasted_iota(jnp.int32, qt.shape, 0)
    qqt = jnp.concatenate([jnp.where(feat < _DIFF_DQ, qt, 0.0), jnp.where(feat >= _DIFF_DQ, qt, 0.0)],
                          axis=1).astype(_BF)

    def rows(j):
        return pl.ds(pl.multiple_of(j * tk, tk), tk)

    def scores(j):
        return _dot(k_ref[rows(j), :], qqt)

    def softmax(s, m, l):
        mn = jnp.maximum(m, jnp.max(s, axis=0, keepdims=True))
        alpha = jnp.exp2(m - mn)
        p = jnp.exp2(s - mn)
        return p.astype(_BF), mn, alpha * l + jnp.sum(p, axis=0, keepdims=True), alpha

    def step(i, slot, carry, with_softmax, with_scores):
        m, l, alpha, acc = carry
        acc = alpha * acc + _dot(vt_ref[:, rows(i)], p_ref[slot])
        if with_softmax:
            pn, m, l, alpha = softmax(s_ref[1 - slot], m, l)
            p_ref[1 - slot] = pn
        if with_scores:
            s_ref[slot] = scores(i + 2)
        return m, l, alpha, acc

    s_ref[0] = scores(0)
    if nb > 1:
        s_ref[1] = scores(1)
    p0, m, l, alpha = softmax(s_ref[0], jnp.full((1, w2), _NEG, _F32), jnp.zeros((1, w2), _F32))
    p_ref[0] = p0
    carry = (m, l, alpha, jnp.zeros((_HD, w2), _F32))
    n_full = max(nb - 2, 0)

    def pair(j, carry):
        carry = step(2 * j, 0, carry, True, True)
        return step(2 * j + 1, 1, carry, True, True)

    carry = lax.fori_loop(0, n_full // 2, pair, carry)
    for i in range(n_full - n_full % 2, nb):
        carry = step(i, i % 2, carry, i + 1 < nb, i + 2 < nb)
    _, l, _, acc = carry
    o = acc / l
    od = o[:, 0:tq] - lam_ref[0] * o[:, tq:w2]
    on = od * lax.rsqrt(jnp.mean(od * od, axis=0, keepdims=True) + _EPS)
    o_ref[...] = (on.T * (nw_ref[...] * out_scale)).astype(o_ref.dtype)


def _diff_kernel(lam_ref, q_ref, k_ref, v_ref, nw_ref, o_ref, vt_ref, s_ref, p_ref, *, n_ctx, tk, out_scale):
    qi = pl.program_id(2)
    S = k_ref.shape[0]

    @pl.when(qi == 0)
    def _():
        def transpose_v(c, _):
            r = pl.ds(pl.multiple_of(c * 128, 128), 128)
            vt_ref[:, r] = v_ref[r, :].astype(_F32).T.astype(_BF)
            return 0
        lax.fori_loop(0, S // 128, transpose_v, 0)
        _diff_ctx_tile(lam_ref, q_ref, k_ref, v_ref, nw_ref, o_ref, n_ctx=n_ctx, out_scale=out_scale)

    @pl.when(qi > 0)
    def _():
        _diff_lat_tile(lam_ref, q_ref, k_ref, vt_ref, nw_ref, o_ref, s_ref, p_ref, tk=tk, out_scale=out_scale)


def _diff(p, lam, norm_w, lambda_init, C):
    B, S, _ = p.shape
    tq = C
    tk = 768 if S % 768 == 0 and S // 768 >= 3 else 256
    return pl.pallas_call(
        functools.partial(_diff_kernel, n_ctx=C, tk=tk, out_scale=1.0 - lambda_init),
        grid=(B, _DIFF_H, S // tq),
        in_specs=[pl.BlockSpec(memory_space=pltpu.SMEM),
                  pl.BlockSpec((None, tq, _HD), lambda b, h, i: (b, i, _CB_DIF_Q + h)),
                  pl.BlockSpec((None, S, _HD), lambda b, h, i: (b, 0, _CB_DIF_K + h)),
                  pl.BlockSpec((None, S, _HD), lambda b, h, i: (b, 0, _CB_DIF_V + h)),
                  pl.BlockSpec((1, _HD), lambda b, h, i: (0, 0))],
        out_specs=pl.BlockSpec((None, tq, _HD), lambda b, h, i: (b, i, h)),
        out_shape=jax.ShapeDtypeStruct((B, S, _DIFF_H * _HD), _BF),
        scratch_shapes=[pltpu.VMEM((_HD, S), _BF), pltpu.VMEM((2, tk, 2 * tq), _F32),
                        pltpu.VMEM((2, tk, 2 * tq), _BF)],
        compiler_params=_params(("arbitrary", "arbitrary", "arbitrary"), 40 * 2 ** 20),
        name="diff_attn",
    )(lam, p, p, p, norm_w.reshape(1, _HD))


def _outproj_kernel(r_ref, s_ref, d_ref, wr_ref, ws_ref, wd_ref, o_ref):
    o_ref[...] = (_dot(r_ref[...], wr_ref[...]) + _dot(s_ref[...], ws_ref[...])
                  + _dot(d_ref[...], wd_ref[...]))


def _outproj(ret_y, swa_y, dif_y, w_bf):
    B, S, _ = ret_y.shape
    D = w_bf.shape[1]
    tm = 768 if S % 768 == 0 else 256
    tn = min(512, D)
    wr, wsw, wd = ret_y.shape[2], swa_y.shape[2], dif_y.shape[2]
    w_r, w_s, w_d = w_bf[0:wr], w_bf[wr:wr + wsw], w_bf[wr + wsw:]
    act = lambda width: pl.BlockSpec((None, tm, width), lambda b, i, j: (b, i, 0))
    wgt = lambda width: pl.BlockSpec((width, tn), lambda b, i, j: (0, j))
    return pl.pallas_call(
        _outproj_kernel,
        grid=(B, S // tm, D // tn),
        in_specs=[act(wr), act(wsw), act(wd), wgt(wr), wgt(wsw), wgt(wd)],
        out_specs=pl.BlockSpec((None, tm, tn), lambda b, i, j: (b, i, j)),
        out_shape=jax.ShapeDtypeStruct((B, S, D), _F32),
        compiler_params=_params(("arbitrary", "arbitrary", "arbitrary"),48 * 2 ** 20),
        name="out_proj",
    )(ret_y, swa_y, dif_y, w_r, w_s, w_d)


def _ln_core(z, g, b):
    mu = jnp.mean(z, axis=-1, keepdims=True)
    d = z - mu
    var = jnp.mean(d * d, axis=-1, keepdims=True)
    return d * lax.rsqrt(var + _EPS) * g + b


def _norm_mod_kernel(x_ref, y_ref, gate_ref, lng_ref, lnb_ref, sc_ref, sh_ref, xo_ref, ho_ref, *, alpha):
    xn = _ln_core(alpha * x_ref[...] + gate_ref[...] * y_ref[...], lng_ref[...], lnb_ref[...])
    xo_ref[...] = xn
    ho_ref[...] = (xn * (1.0 + sc_ref[...]) + sh_ref[...]).astype(ho_ref.dtype)


def _norm_mod(x, y, modv, l_gate, gate_chunk, l_next, lng, lnb, alpha, h_dtype):
    B, S, D = x.shape
    tr = _ROW_TILE
    row = pl.BlockSpec((None, tr, D), lambda b, t: (b, t, 0))
    vec = lambda l, chunk: pl.BlockSpec((None, 1, D), _mod_index(l, chunk))
    par = pl.BlockSpec((1, D), lambda b, t: (0, 0))
    l_n, sh_chunk, sc_chunk = l_next
    return pl.pallas_call(
        functools.partial(_norm_mod_kernel, alpha=alpha),
        grid=(B, S // tr),
        in_specs=[row, row, vec(l_gate, gate_chunk), par, par, vec(l_n, sc_chunk), vec(l_n, sh_chunk)],
        out_specs=[row, row],
        out_shape=[jax.ShapeDtypeStruct((B, S, D), _F32), jax.ShapeDtypeStruct((B, S, D), h_dtype)],
        compiler_params=_params(("arbitrary", "arbitrary"),48 * 2 ** 20),
        name="norm_mod",
    )(x, y, modv, lng.reshape(1, D), lnb.reshape(1, D), modv, modv)


def _split_bf(x):
    hi = x.astype(_BF)
    lo = (x - hi.astype(_F32)).astype(_BF)
    return hi, lo


def _router_kernel(h_ref, wt_ref, b_ref, idx_ref, rank_ref, gw_ref, cnt_ref, carry_ref):
    t = pl.program_id(0)
    tr = h_ref.shape[0]

    @pl.when(t == 0)
    def _():
        carry_ref[...] = jnp.zeros_like(carry_ref)

    h_hi, h_lo = _split_bf(h_ref[...])
    w_hi, w_lo = _split_bf(wt_ref[...])
    logits = _dot_nt(w_hi, h_hi) + _dot_nt(w_hi, h_lo) + _dot_nt(w_lo, h_hi) + b_ref[...]
    e_iota = lax.broadcasted_iota(jnp.int32, logits.shape, 0)
    tops, sels = [], []
    cur = logits
    for _ in range(_TOP_K):
        m = jnp.max(cur, axis=0, keepdims=True)
        first = jnp.min(jnp.where(cur == m, e_iota, _N_EXP), axis=0, keepdims=True)
        sel = e_iota == first
        tops.append(m)
        sels.append(sel)
        idx_ref[len(tops) - 1:len(tops), :] = first
        cur = jnp.where(sel, -jnp.inf, cur)
    ex = [jnp.exp(v - tops[0]) for v in tops]
    den = ex[0] + ex[1] + ex[2] + ex[3]
    for k in range(_TOP_K):
        gw_ref[k:k + 1, :] = ex[k] / den
    member = jnp.zeros(logits.shape, _F32)
    for sel in sels:
        member = member + jnp.where(sel, 1.0, 0.0)
    ri = lax.broadcasted_iota(jnp.int32, (tr, tr), 0)
    ci = lax.broadcasted_iota(jnp.int32, (tr, tr), 1)
    upper = jnp.where(ri < ci, 1.0, 0.0).astype(_BF)
    before = _dot(member.astype(_BF), upper) + carry_ref[:, 0:1]
    for k in range(_TOP_K):
        rank_ref[k:k + 1, :] = jnp.sum(jnp.where(sels[k], before, 0.0), axis=0, keepdims=True).astype(jnp.int32)
    carry_ref[...] = carry_ref[...] + jnp.sum(member, axis=1, keepdims=True)
    cnt_ref[...] = carry_ref[...]


def _router(h2d, w_router, b_router):
    T, D = h2d.shape
    tr = 256
    k_out = lambda dt: jax.ShapeDtypeStruct((_TOP_K, T), dt)
    k_spec = pl.BlockSpec((_TOP_K, tr), lambda t: (0, t))
    return pl.pallas_call(
        _router_kernel,
        grid=(T // tr,),
        in_specs=[pl.BlockSpec((tr, D), lambda t: (t, 0)),
                  pl.BlockSpec((_N_EXP, D), lambda t: (0, 0)),
                  pl.BlockSpec((_N_EXP, 1), lambda t: (0, 0))],
        out_specs=[k_spec, k_spec, k_spec, pl.BlockSpec((_N_EXP, 128), lambda t: (0, 0))],
        out_shape=[k_out(jnp.int32), k_out(jnp.int32), k_out(_F32),
                   jax.ShapeDtypeStruct((_N_EXP, 128), _F32)],
        scratch_shapes=[pltpu.VMEM((_N_EXP, 128), _F32)],
        compiler_params=_params(("arbitrary",), 32 * 2 ** 20),
        name="moe_router",
    )(h2d, w_router.T, b_router.reshape(_N_EXP, 1))


def _regroup_kernel(w_ref, perm_ref, o_ref):
    w = w_ref[...].astype(_BF)
    for g in range(w.shape[1] // 256):
        cols = slice(g * 256, (g + 1) * 256)
        o_ref[:, cols] = _dot(w[:, cols], perm_ref[...]).astype(o_ref.dtype)


def _regroup_perm():
    perm = np.zeros((256, 256), np.float32)
    perm[2 * np.arange(128), np.arange(128)] = 1.0
    perm[2 * np.arange(128) + 1, 128 + np.arange(128)] = 1.0
    return perm


def _regroup_gate_up(w):
    E, D, F2 = w.shape
    tk = min(1024, D)
    return pl.pallas_call(
        _regroup_kernel,
        grid=(E, D // tk),
        in_specs=[pl.BlockSpec((None, tk, F2), lambda e, k: (e, k, 0)),
                  pl.BlockSpec((256, 256), lambda e, k: (0, 0))],
        out_specs=pl.BlockSpec((None, tk, F2), lambda e, k: (e, k, 0)),
        out_shape=jax.ShapeDtypeStruct((E, D, F2), _BF),
        compiler_params=_params(("arbitrary", "arbitrary"),32 * 2 ** 20),
        name="moe_regroup",
    )(w, jnp.asarray(_regroup_perm(), _BF))


def _gate_up_kernel(te_ref, na_ref, src_ref, nxt_ref, x_hbm, w_ref, b_ref, o_ref, xbuf_ref, sem):
    i = pl.program_id(0)
    tm = o_ref.shape[0]
    slot = lax.rem(i, 2)

    def start_rows(idx_ref, to_slot):
        def issue(r, _):
            pltpu.make_async_copy(x_hbm.at[pl.ds(idx_ref[0, r], 1)], xbuf_ref.at[to_slot, pl.ds(r, 1)],
                                  sem.at[to_slot]).start()
            return 0
        lax.fori_loop(0, tm, issue, 0, unroll=8)

    @pl.when(i == 0)
    def _():
        start_rows(src_ref, 0)

    @pl.when(i + 1 < na_ref[0])
    def _():
        start_rows(nxt_ref, 1 - slot)

    @pl.when(i < na_ref[0])
    def _():
        pltpu.make_async_copy(x_hbm.at[pl.ds(0, tm)], xbuf_ref.at[slot], sem.at[slot]).wait()
        gu = _dot(xbuf_ref[slot].astype(_BF), w_ref[...]) + b_ref[...]
        for g in range(o_ref.shape[1] // 128):
            gate = jnp.minimum(gu[:, g * 256:g * 256 + 128], _LIMIT)
            up = jnp.clip(gu[:, g * 256 + 128:(g + 1) * 256], -_LIMIT, _LIMIT)
            act = (up + 1.0) * gate * jax.nn.sigmoid(_SWIGLU_ALPHA * gate)
            o_ref[:, g * 128:(g + 1) * 128] = act.astype(o_ref.dtype)

    @pl.when(i >= na_ref[0])
    def _():
        o_ref[...] = jnp.zeros_like(o_ref)


def _down_kernel(te_ref, na_ref, a_ref, w_ref, b_ref, o_ref):
    i = pl.program_id(0)

    @pl.when(i < na_ref[0])
    def _():
        o_ref[...] = _dot(a_ref[...], w_ref[...]) + b_ref[...]

    @pl.when(i >= na_ref[0])
    def _():
        o_ref[...] = jnp.zeros_like(o_ref)


def _expert_ffn(x2d, src, tile_expert, n_active, w_gu, b_gu, w_dn, b_dn):
    D = x2d.shape[1]
    P = src.shape[0]
    E, _, F2 = w_gu.shape
    F = F2 // 2
    tm = _MOE_TM
    nt = P // tm
    src3 = src.reshape(nt, 1, tm)
    row = lambda width: pl.BlockSpec((tm, width), lambda i, te, na: (jnp.minimum(i, na[0] - 1), 0))
    act = pl.pallas_call(
        _gate_up_kernel,
        grid_spec=pltpu.PrefetchScalarGridSpec(
            num_scalar_prefetch=2, grid=(nt,),
            in_specs=[pl.BlockSpec((None, 1, tm), lambda i, te, na: (i, 0, 0), memory_space=pltpu.SMEM),
                      pl.BlockSpec((None, 1, tm), lambda i, te, na: (jnp.minimum(i + 1, nt - 1), 0, 0),
                                   memory_space=pltpu.SMEM),
                      pl.BlockSpec(memory_space=pl.ANY),
                      pl.BlockSpec((None, D, F2), lambda i, te, na: (te[i], 0, 0)),
                      pl.BlockSpec((None, 1, F2), lambda i, te, na: (te[i], 0, 0))],
            out_specs=pl.BlockSpec((tm, F), lambda i, te, na: (i, 0)),
            scratch_shapes=[pltpu.VMEM((2, tm, D), x2d.dtype), pltpu.SemaphoreType.DMA((2,))]),
        out_shape=jax.ShapeDtypeStruct((P, F), _BF),
        compiler_params=_params(("arbitrary",), _VMEM_CAP),
        name="moe_gate_up",
    )(tile_expert, n_active, src3, src3, x2d, w_gu, b_gu.reshape(E, 1, F2))
    return pl.pallas_call(
        _down_kernel,
        grid_spec=pltpu.PrefetchScalarGridSpec(
            num_scalar_prefetch=2, grid=(nt,),
            in_specs=[row(F),
                      pl.BlockSpec((None, F, D), lambda i, te, na: (te[i], 0, 0)),
                      pl.BlockSpec((None, 1, D), lambda i, te, na: (te[i], 0, 0))],
            out_specs=pl.BlockSpec((tm, D), lambda i, te, na: (i, 0))),
        out_shape=jax.ShapeDtypeStruct((P, D), _F32),
        compiler_params=_params(("arbitrary",), _VMEM_CAP),
        name="moe_down",
    )(tile_expert, n_active, act, w_dn, b_dn.reshape(E, 1, D))


def _combine_kernel(dest_ref, nxt_ref, y_hbm, gw_ref, x_ref, gate_ref, lng_ref, lnb_ref, *rest, alpha, with_h):
    if with_h:
        sc_ref, sh_ref, xo_ref, ho_ref, buf_ref, sem = rest
    else:
        xo_ref, buf_ref, sem = rest
    g = pl.program_id(0)
    tc = x_ref.shape[0]
    slot = lax.rem(g, 2)

    def start_rows(idx_ref, to_slot):
        def issue(r, _):
            for k in range(_TOP_K):
                pltpu.make_async_copy(y_hbm.at[pl.ds(idx_ref[0, k * tc + r], 1)],
                                      buf_ref.at[to_slot, k, pl.ds(r, 1)], sem.at[to_slot]).start()
            return 0
        lax.fori_loop(0, tc, issue, 0, unroll=4)

    @pl.when(g == 0)
    def _():
        start_rows(dest_ref, 0)

    @pl.when(g + 1 < pl.num_programs(0))
    def _():
        start_rows(nxt_ref, 1 - slot)

    for k in range(_TOP_K):
        pltpu.make_async_copy(y_hbm.at[pl.ds(0, tc)], buf_ref.at[slot, k], sem.at[slot]).wait()
    gw = gw_ref[...]
    f = gw[:, 0:1] * buf_ref[slot, 0]
    for k in range(1, _TOP_K):
        f = f + gw[:, k:k + 1] * buf_ref[slot, k]
    xn = _ln_core(alpha * x_ref[...] + gate_ref[...] * f, lng_ref[...], lnb_ref[...])
    xo_ref[...] = xn
    if with_h:
        ho_ref[...] = (xn * (1.0 + sc_ref[...]) + sh_ref[...]).astype(ho_ref.dtype)


def _combine_norm(ys, dest, gw, x, modv, l, l_next, lng, lnb, alpha, n_ctx):
    B, S, D = x.shape
    tc = 128
    tpb = S // tc
    with_h = l_next is not None
    first = 0 if with_h else n_ctx // tc
    per = tpb - first
    n_tiles = B * per
    ctx_tiles = n_ctx // tc
    dest_t = (dest.reshape(_TOP_K, B, tpb, tc)[:, :, first:, :].transpose(1, 2, 0, 3)
              .reshape(n_tiles, 1, _TOP_K * tc))
    bt = lambda g: (g // per, first + g % per)

    def mod_index(l_, chunk):
        def index(g):
            b, t = bt(g)
            return ((l_ * 8 + jnp.where(t < ctx_tiles, 2, b)) * 6 + chunk, 0, 0)
        return index

    row_in = pl.BlockSpec((None, tc, D), lambda g: (*bt(g), 0))
    row_out = pl.BlockSpec((None, tc, D), lambda g: (g // per, g % per, 0))
    vec = lambda l_, chunk: pl.BlockSpec((None, 1, D), mod_index(l_, chunk))
    par = pl.BlockSpec((1, D), lambda g: (0, 0))
    idx_spec = lambda shift: pl.BlockSpec((None, 1, _TOP_K * tc),
                                          lambda g: (jnp.minimum(g + shift, n_tiles - 1), 0, 0),
                                          memory_space=pltpu.SMEM)
    in_specs = [idx_spec(0), idx_spec(1), pl.BlockSpec(memory_space=pl.ANY),
                pl.BlockSpec((tc, _TOP_K), lambda g: (bt(g)[0] * tpb + bt(g)[1], 0)),
                row_in, vec(l, 5), par, par]
    args = [dest_t, dest_t, ys, gw, x, modv, lng.reshape(1, D), lnb.reshape(1, D)]
    out_specs = [row_out]
    out_shape = [jax.ShapeDtypeStruct((B, per * tc, D), _F32)]
    if with_h:
        l_n, sh_chunk, sc_chunk = l_next
        in_specs += [vec(l_n, sc_chunk), vec(l_n, sh_chunk)]
        args += [modv, modv]
        out_specs.append(row_out)
        out_shape.append(jax.ShapeDtypeStruct((B, S, D), _BF))
    return pl.pallas_call(
        functools.partial(_combine_kernel, alpha=alpha, with_h=with_h),
        grid=(n_tiles,),
        in_specs=in_specs,
        out_specs=out_specs,
        out_shape=out_shape,
        scratch_shapes=[pltpu.VMEM((2, _TOP_K, tc, D), _F32), pltpu.SemaphoreType.DMA((2,))],
        compiler_params=_params(("arbitrary",), 48 * 2 ** 20),
        name="moe_combine_norm",
    )(*args)


def _moe_plan(idx, rank, counts, T):
    tm = _MOE_TM
    nt = -(-_TOP_K * T // tm) + _N_EXP
    cnt = counts[:, 0].astype(jnp.int32)
    padded = (cnt + tm - 1) // tm * tm
    ends = jnp.cumsum(padded)
    starts = ends - padded
    experts = jnp.arange(_N_EXP, dtype=jnp.int32)
    dest = jnp.sum(jnp.where(idx[..., None] == experts, starts, 0), axis=-1) + rank
    n_active = (ends[-1] // tm).astype(jnp.int32)
    tile_start = jnp.minimum(jnp.arange(nt, dtype=jnp.int32), n_active - 1) * tm
    te = jnp.sum((ends[None, :] <= tile_start[:, None]).astype(jnp.int32), axis=1)
    tok = jnp.broadcast_to(jnp.arange(T, dtype=jnp.int32)[None, :], dest.shape)
    src = jnp.zeros((nt * tm,), jnp.int32).at[dest.reshape(-1)].set(tok.reshape(-1))
    return dest, src, te, n_active.reshape(1)


def kernel(x, c, ctx, c_ctx, w_ada, b_ada, w_in, w_out, ret_decay_f, ret_decay_b, ret_norm_w, swa_sink,
           diff_lambda, diff_norm_w, ln1_g, ln1_b, ln2_g, ln2_b, w_router, b_router, w_gate_up, b_gate_up,
           w_down, b_down):
    B, N, D = x.shape
    C = ctx.shape[1]
    S = C + N
    L = w_in.shape[0]
    T = B * S
    alpha = (2 * L) ** 0.25
    u = jnp.concatenate([ctx, x], axis=1)
    modv = _ada_mods(c, c_ctx, w_ada, b_ada)
    tabs = _rope_tables(S, C)
    h = _modulate(u, modv, 0)
    for l in range(L):
        lambda_init = 0.8 - 0.6 * math.exp(-0.3 * l)
        p = _inproj(h, w_in[l].astype(_BF), tabs, C)
        lg = jnp.stack([jax.nn.log_sigmoid(ret_decay_f[l].astype(_F32)),
                        jax.nn.log_sigmoid(ret_decay_b[l].astype(_F32))])
        lp = diff_lambda[l].astype(_F32)
        lam = (jnp.exp(jnp.sum(lp[0] * lp[1])) - jnp.exp(jnp.sum(lp[2] * lp[3])) + lambda_init).reshape(1)
        ret_y = _retention(p, lg, ret_norm_w[l], C)
        swa_y = _swa(p, swa_sink[l].astype(_F32), C)
        dif_y = _diff(p, lam, diff_norm_w[l], lambda_init, C)
        y = _outproj(ret_y, swa_y, dif_y, w_out[l].astype(_BF))
        u, h2 = _norm_mod(u, y, modv, l, 2, (l, 3, 4), ln1_g[l], ln1_b[l], alpha, _F32)
        h2d = h2.reshape(T, D)
        idx, rank, gw, counts = _router(h2d, w_router[l], b_router[l])
        dest, src, te, n_active = _moe_plan(idx, rank, counts, T)
        E, F2 = b_gate_up.shape[1], b_gate_up.shape[2]
        w_gu = _regroup_gate_up(w_gate_up[l])
        b_gu = b_gate_up[l].reshape(E, F2 // 256, 128, 2).transpose(0, 1, 3, 2).reshape(E, F2)
        ys = _expert_ffn(h2d, src, te, n_active, w_gu, b_gu, w_down[l].astype(_BF), b_down[l])
        if l + 1 < L:
            u, h = _combine_norm(ys, dest, gw.T, u, modv, l, (l + 1, 0, 1), ln2_g[l], ln2_b[l], alpha, C)
        else:
            (u,) = _combine_norm(ys, dest, gw.T, u, modv, l, None, ln2_g[l], ln2_b[l], alpha, C)
    return u
```
